```python
import math
import jax, jax.numpy as jnp
from jax import lax
import numpy as np

D_MODEL = 1024
BATCH = 8
SEQ = 2048
DEPTH = 4

N_META = 16
RMS_EPS = 1e-6
L2_EPS = 1e-6
CONV_K = 4
SB_HEADS = 8
SB_HEAD_DIM = 128
SB_WIDTH = SB_HEADS * SB_HEAD_DIM
SB_BLOCK = 128
GDN_HEADS = 8
GDN_DK = 128
GDN_DV = 128
GDN_QK_WIDTH = GDN_HEADS * GDN_DK
GDN_V_WIDTH = GDN_HEADS * GDN_DV
GDN_CHUNK = 64
SSM_EXPAND = 2
SSM_INNER = SSM_EXPAND * D_MODEL
SSM_HEAD_DIM = 64
SSM_HEADS = SSM_INNER // SSM_HEAD_DIM
SSM_GROUPS = 2
SSM_STATE = 128
SSM_CHUNK = 64
SSM_CONV_CH = SSM_INNER + 2 * SSM_GROUPS * SSM_STATE
N_BRANCH = 3
IN_SPLITS = (SB_WIDTH, SB_WIDTH, SB_WIDTH, SB_WIDTH,
             GDN_QK_WIDTH, GDN_QK_WIDTH, GDN_V_WIDTH, GDN_V_WIDTH, GDN_HEADS, GDN_HEADS,
             SSM_INNER, SSM_CONV_CH, SSM_HEADS,
             N_BRANCH * D_MODEL)
D_IN = 4 * SB_WIDTH + 2 * GDN_QK_WIDTH + 2 * GDN_V_WIDTH + 2 * GDN_HEADS + SSM_INNER + SSM_CONV_CH + SSM_HEADS + N_BRANCH * D_MODEL

kernel_name = 'hybrid_stickbreak_gdn_ssd_block'


def _split(t, sizes):
    out, start = [], 0
    for s in sizes:
        out.append(t[..., start:start + s])
        start += s
    return out


def _rmsnorm(x, g):
    xf = x.astype(jnp.float32)
    y = xf * lax.rsqrt(jnp.mean(xf * xf, axis=-1, keepdims=True) + RMS_EPS)
    return (y * g.astype(jnp.float32)).astype(x.dtype)


def _l2norm(t):
    t = t.astype(jnp.float32)
    return t * lax.rsqrt(jnp.sum(t * t, axis=-1, keepdims=True) + L2_EPS)


def _front_pad(t, n):
    return jnp.pad(t, [(0, 0), (n, 0)] + [(0, 0)] * (t.ndim - 2))


def _causal_dwconv(x, w):
    k, c = w.shape
    return lax.conv_general_dilated(x, w.astype(x.dtype).reshape(k, 1, c), window_strides=(1,),
                                    padding=[(k - 1, 0)], dimension_numbers=('NWC', 'WIO', 'NWC'),
                                    feature_group_count=c)


def _stick_breaking_attention(q, k, v):
    bsz, seq, nh, dh = q.shape
    pad = SB_BLOCK - N_META
    lp = seq + pad
    nb = lp // SB_BLOCK
    scale = dh ** -0.5
    qf, kf, vf = [jnp.swapaxes(_front_pad(t.astype(jnp.float32), pad), 1, 2) for t in (q, k, v)]
    q_blocks = jnp.moveaxis(qf.reshape(bsz, nh, nb, SB_BLOCK, dh), 2, 0)
    key_pos = jnp.arange(lp)

    def one_block(args):
        qb, bi = args
        q_pos = bi * SB_BLOCK + jnp.arange(SB_BLOCK)
        mask = (key_pos[None, :] < q_pos[:, None]) & (key_pos[None, :] >= pad)
        z = jnp.einsum('bhqd,bhkd->bhqk', qb, kf) * scale
        log_keep = jnp.where(mask, jax.nn.log_sigmoid(-z), 0.0)
        later = lax.cumsum(log_keep, axis=3, reverse=True) - log_keep
        w = jnp.where(mask, jnp.exp(jax.nn.log_sigmoid(z) + later), 0.0)
        return jnp.einsum('bhqk,bhkd->bhqd', w, vf)

    out = lax.map(one_block, (q_blocks, jnp.arange(nb)))
    out = jnp.transpose(out, (1, 0, 3, 2, 4)).reshape(bsz, lp, nh, dh)
    return out[:, pad:]


def _gated_delta_rule(q, k, v, g, beta):
    pad = GDN_CHUNK - N_META
    q, k, v, g, beta = [_front_pad(t, pad) for t in (q, k, v, g, beta)]
    bsz, lp, nh, dk = q.shape
    dv = v.shape[-1]
    cl = GDN_CHUNK
    nc = lp // cl

    def chunked(t):
        return t.reshape((bsz, nc, cl) + t.shape[2:])

    q = chunked(q * dk ** -0.5)
    k, v, g, beta = chunked(k), chunked(v), chunked(g), chunked(beta)
    gc = jnp.cumsum(g, axis=2)
    gt = jnp.moveaxis(gc, 2, 3)
    seg = gt[..., :, None] - gt[..., None, :]
    idx = jnp.arange(cl)
    strict = idx[:, None] > idx[None, :]
    incl = idx[:, None] >= idx[None, :]
    dec_strict = jnp.exp(jnp.where(strict, seg, -jnp.inf))
    dec_incl = jnp.exp(jnp.where(incl, seg, -jnp.inf))
    kb = k * beta[..., None]
    m = jnp.einsum('bclhd,bcshd->bchls', kb, k) * dec_strict
    eye = jnp.eye(cl, dtype=m.dtype)
    t_inv = lax.linalg.triangular_solve(m + eye, jnp.broadcast_to(eye, m.shape), left_side=True,
                                        lower=True, unit_diagonal=True)
    u = jnp.einsum('bchls,bcshd->bclhd', t_inv, v * beta[..., None])
    w = jnp.einsum('bchls,bcshd->bclhd', t_inv, kb * jnp.exp(gc)[..., None])
    a_qk = jnp.einsum('bclhd,bcshd->bchls', q, k) * dec_incl
    g_last = gc[:, :, -1]
    q_dec = q * jnp.exp(gc)[..., None]
    k_end = k * jnp.exp(g_last[:, :, None] - gc)[..., None]

    def step(state, inp):
        qd, ke, uc, wc, aqk, gl = inp
        v_new = uc - jnp.einsum('blhk,bhkv->blhv', wc, state)
        o = jnp.einsum('blhk,bhkv->blhv', qd, state) + jnp.einsum('bhls,bshv->blhv', aqk, v_new)
        state = state * jnp.exp(gl)[..., None, None] + jnp.einsum('blhk,blhv->bhkv', ke, v_new)
        return state, o

    s0 = jnp.zeros((bsz, nh, dk, dv), q.dtype)
    xs = tuple(jnp.moveaxis(t, 1, 0) for t in (q_dec, k_end, u, w, a_qk, g_last))
    _, o = lax.scan(step, s0, xs)
    o = jnp.moveaxis(o, 0, 1).reshape(bsz, lp, nh, dv)
    return o[:, pad:]


def _ssd_scan(x, dt, a, b_in, c_in):
    pad = SSM_CHUNK - N_META
    x, dt, b_in, c_in = [_front_pad(t, pad) for t in (x, dt, b_in, c_in)]
    bsz, lp, nh, p = x.shape
    ng, n = b_in.shape[2], b_in.shape[3]
    hg = nh // ng
    cl = SSM_CHUNK
    nc = lp // cl
    xs = (x * dt[..., None]).reshape(bsz, nc, cl, ng, hg, p)
    la = (dt * a).reshape(bsz, nc, cl, ng, hg)
    bc = b_in.reshape(bsz, nc, cl, ng, n)
    cc = c_in.reshape(bsz, nc, cl, ng, n)
    cs = jnp.cumsum(la, axis=2)
    causal = jnp.tril(jnp.ones((cl, cl), dtype=bool))
    seg = cs[:, :, :, None] - cs[:, :, None, :]
    decay = jnp.exp(jnp.where(causal[:, :, None, None], seg, -jnp.inf))
    scores = jnp.einsum('bclgn,bcsgn->bclsg', cc, bc)[..., None] * decay
    y_diag = jnp.einsum('bclsgh,bcsghp->bclghp', scores, xs)
    to_end = jnp.exp(cs[:, :, -1:] - cs)
    states = jnp.einsum('bclgn,bclghp->bcghpn', bc, xs * to_end[..., None])
    chunk_decay = jnp.exp(cs[:, :, -1])

    def step(hst, inp):
        st, cd = inp
        return cd[..., None, None] * hst + st, hst

    h0 = jnp.zeros((bsz, ng, hg, p, n), xs.dtype)
    _, h_prev = lax.scan(step, h0, (jnp.moveaxis(states, 1, 0), jnp.moveaxis(chunk_decay, 1, 0)))
    h_prev = jnp.moveaxis(h_prev, 0, 1)
    y_off = jnp.einsum('bclgn,bcghpn->bclghp', cc, h_prev) * jnp.exp(cs)[..., None]
    y = (y_diag + y_off).reshape(bsz, lp, nh, p)
    return y[:, pad:]


def _hybrid_mixer(u, w_in, gdn_conv_w, gdn_a_log, gdn_dt_bias, gdn_norm_g, ssm_conv_w, ssm_conv_b,
                  ssm_a_log, ssm_dt_bias, ssm_d, ssm_norm_g, w_branch_a, w_branch_b, w_branch_c, w_out):
    f32 = jnp.float32
    dtype = u.dtype
    bsz, seq, _ = u.shape
    proj = u @ w_in
    (sb_q, sb_k, sb_v, sb_z, gd_q, gd_k, gd_v, gd_z, gd_b, gd_a,
     ss_z, ss_xbc, ss_dt, gates) = _split(proj, IN_SPLITS)

    def heads(t, nh):
        return t.reshape(bsz, seq, nh, -1)

    o_a = _stick_breaking_attention(heads(sb_q, SB_HEADS), heads(sb_k, SB_HEADS), heads(sb_v, SB_HEADS))
    o_a = o_a.reshape(bsz, seq, SB_WIDTH).astype(dtype) * jax.nn.silu(sb_z)

    qkv = jax.nn.silu(_causal_dwconv(jnp.concatenate([gd_q, gd_k, gd_v], axis=-1), gdn_conv_w))
    cq, ck, cv = _split(qkv, (GDN_QK_WIDTH, GDN_QK_WIDTH, GDN_V_WIDTH))
    beta = jax.nn.sigmoid(gd_b.astype(f32))
    g = -jnp.exp(gdn_a_log.astype(f32)) * jax.nn.softplus(gd_a.astype(f32) + gdn_dt_bias.astype(f32))
    o_b = _gated_delta_rule(_l2norm(heads(cq, GDN_HEADS)), _l2norm(heads(ck, GDN_HEADS)),
                            heads(cv, GDN_HEADS).astype(f32), g, beta)
    o_b = _rmsnorm(o_b, gdn_norm_g).reshape(bsz, seq, GDN_V_WIDTH).astype(dtype) * jax.nn.silu(gd_z)

    xbc = jax.nn.silu(_causal_dwconv(ss_xbc, ssm_conv_w) + ssm_conv_b)
    sx, sb, sc = _split(xbc, (SSM_INNER, SSM_GROUPS * SSM_STATE, SSM_GROUPS * SSM_STATE))
    dt = jax.nn.softplus(ss_dt.astype(f32) + ssm_dt_bias.astype(f32))
    a = -jnp.exp(ssm_a_log.astype(f32))
    xh = heads(sx, SSM_HEADS).astype(f32)
    y = _ssd_scan(xh, dt, a, heads(sb, SSM_GROUPS).astype(f32), heads(sc, SSM_GROUPS).astype(f32))
    y = y + ssm_d.astype(f32)[:, None] * xh
    y = y.reshape(bsz, seq, SSM_INNER) * jax.nn.silu(ss_z.astype(f32))
    o_c = _rmsnorm(y.reshape(bsz, seq, SSM_GROUPS, -1), ssm_norm_g.reshape(SSM_GROUPS, -1))
    o_c = o_c.reshape(bsz, seq, SSM_INNER).astype(dtype)

    g_a, g_b, g_c = _split(jax.nn.sigmoid(gates), (D_MODEL, D_MODEL, D_MODEL))
    merged = g_a * (o_a @ w_branch_a) + g_b * (o_b @ w_branch_b) + g_c * (o_c @ w_branch_c)
    return merged @ w_out


def _fwd_setup_inputs(seed: int = 0) -> dict:
    key = jax.random.key(seed)
    ks = jax.random.split(key, 20)
    f32 = jnp.float32

    def nrm(k, shape, scale):
        return jax.random.normal(k, shape, f32) * scale

    def gain(k, shape):
        return 1.0 + 0.02 * jax.random.normal(k, shape, f32)

    def dt_bias(k, shape):
        dt = jnp.exp(jax.random.uniform(k, shape, f32, math.log(1e-3), math.log(1e-1)))
        return dt + jnp.log(-jnp.expm1(-dt))

    def a_log(k, shape):
        return jnp.log(jax.random.uniform(k, shape, f32, 1.0, 16.0))

    return {
        'x': jax.random.normal(ks[0], (BATCH, SEQ, D_MODEL), f32),
        'meta_tokens': nrm(ks[1], (N_META, D_MODEL), 1.0),
        'norm_g': gain(ks[2], (DEPTH, D_MODEL)),
        'w_in': nrm(ks[3], (DEPTH, D_MODEL, D_IN), D_MODEL ** -0.5),
        'gdn_conv_w': nrm(ks[4], (DEPTH, CONV_K, 2 * GDN_QK_WIDTH + GDN_V_WIDTH), CONV_K ** -0.5),
        'gdn_a_log': a_log(ks[5], (DEPTH, GDN_HEADS)),
        'gdn_dt_bias': dt_bias(ks[6], (DEPTH, GDN_HEADS)),
        'gdn_norm_g': gain(ks[7], (DEPTH, GDN_DV)),
        'ssm_conv_w': nrm(ks[8], (DEPTH, CONV_K, SSM_CONV_CH), CONV_K ** -0.5),
        'ssm_conv_b': nrm(ks[9], (DEPTH, SSM_CONV_CH), 0.01),
        'ssm_a_log': a_log(ks[10], (DEPTH, SSM_HEADS)),
        'ssm_dt_bias': dt_bias(ks[11], (DEPTH, SSM_HEADS)),
        'ssm_d': 1.0 + 0.1 * jax.random.normal(ks[12], (DEPTH, SSM_HEADS), f32),
        'ssm_norm_g': gain(ks[13], (DEPTH, SSM_INNER)),
        'w_branch_a': nrm(ks[14], (DEPTH, SB_WIDTH, D_MODEL), SB_WIDTH ** -0.5),
        'w_branch_b': nrm(ks[15], (DEPTH, GDN_V_WIDTH, D_MODEL), GDN_V_WIDTH ** -0.5),
        'w_branch_c': nrm(ks[16], (DEPTH, SSM_INNER, D_MODEL), SSM_INNER ** -0.5),
        'w_out': nrm(ks[17], (DEPTH, D_MODEL, D_MODEL), D_MODEL ** -0.5),
        'final_norm_g': gain(ks[18], (D_MODEL,)),
    }


def _fwd_reference(x, meta_tokens, norm_g, w_in, gdn_conv_w, gdn_a_log, gdn_dt_bias, gdn_norm_g, ssm_conv_w,
              ssm_conv_b, ssm_a_log, ssm_dt_bias, ssm_d, ssm_norm_g, w_branch_a, w_branch_b, w_branch_c,
              w_out, final_norm_g):
    bsz = x.shape[0]
    meta = jnp.broadcast_to(meta_tokens.astype(x.dtype)[None], (bsz, N_META, D_MODEL))
    h = jnp.concatenate([meta, x], axis=1)
    for layer in range(DEPTH):
        h = h + _hybrid_mixer(_rmsnorm(h, norm_g[layer]), w_in[layer], gdn_conv_w[layer], gdn_a_log[layer],
                              gdn_dt_bias[layer], gdn_norm_g[layer], ssm_conv_w[layer], ssm_conv_b[layer],
                              ssm_a_log[layer], ssm_dt_bias[layer], ssm_d[layer], ssm_norm_g[layer],
                              w_branch_a[layer], w_branch_b[layer], w_branch_c[layer], w_out[layer])
    return _rmsnorm(h, final_norm_g)[:, N_META:]


import jax as _jax
import jax.numpy as _jnp

TWIN_FORMAT = 'train_step'
FWD_PARAMS = ['x', 'meta_tokens', 'norm_g', 'w_in', 'gdn_conv_w', 'gdn_a_log', 'gdn_dt_bias', 'gdn_norm_g', 'ssm_conv_w', 'ssm_conv_b', 'ssm_a_log', 'ssm_dt_bias', 'ssm_d', 'ssm_norm_g', 'w_branch_a', 'w_branch_b', 'w_branch_c', 'w_out', 'final_norm_g']
TWIN_WEIGHTS = ['meta_tokens', 'norm_g', 'w_in', 'gdn_conv_w', 'gdn_a_log', 'gdn_dt_bias', 'gdn_norm_g', 'ssm_conv_w', 'ssm_conv_b', 'ssm_a_log', 'ssm_dt_bias', 'ssm_d', 'ssm_norm_g', 'w_branch_a', 'w_branch_b', 'w_branch_c', 'w_out', 'final_norm_g']
TWIN_DIFF_INPUT = 'x'
TWIN_INPUTS = ['x', 'meta_tokens', 'norm_g', 'w_in', 'gdn_conv_w', 'gdn_a_log', 'gdn_dt_bias', 'gdn_norm_g', 'ssm_conv_w', 'ssm_conv_b', 'ssm_a_log', 'ssm_dt_bias', 'ssm_d', 'ssm_norm_g', 'w_branch_a', 'w_branch_b', 'w_branch_c', 'w_out', 'final_norm_g', 'loss_target', 'm_meta_tokens', 'm_norm_g', 'm_w_in', 'm_gdn_conv_w', 'm_gdn_a_log', 'm_gdn_dt_bias', 'm_gdn_norm_g', 'm_ssm_conv_w', 'm_ssm_conv_b', 'm_ssm_a_log', 'm_ssm_dt_bias', 'm_ssm_d', 'm_ssm_norm_g', 'm_w_branch_a', 'm_w_branch_b', 'm_w_branch_c', 'm_w_out', 'm_final_norm_g', 'v_meta_tokens', 'v_norm_g', 'v_w_in', 'v_gdn_conv_w', 'v_gdn_a_log', 'v_gdn_dt_bias', 'v_gdn_norm_g', 'v_ssm_conv_w', 'v_ssm_conv_b', 'v_ssm_a_log', 'v_ssm_dt_bias', 'v_ssm_d', 'v_ssm_norm_g', 'v_w_branch_a', 'v_w_branch_b', 'v_w_branch_c', 'v_w_out', 'v_final_norm_g']
TWIN_OUTPUTS = ['loss', 'grad_x', 'grad_meta_tokens', 'grad_norm_g', 'grad_w_in', 'grad_gdn_conv_w', 'grad_gdn_a_log', 'grad_gdn_dt_bias', 'grad_gdn_norm_g', 'grad_ssm_conv_w', 'grad_ssm_conv_b', 'grad_ssm_a_log', 'grad_ssm_dt_bias', 'grad_ssm_d', 'grad_ssm_norm_g', 'grad_w_branch_a', 'grad_w_branch_b', 'grad_w_branch_c', 'grad_w_out', 'grad_final_norm_g', 'delta_meta_tokens', 'delta_norm_g', 'delta_w_in', 'delta_gdn_conv_w', 'delta_gdn_a_log', 'delta_gdn_dt_bias', 'delta_gdn_norm_g', 'delta_ssm_conv_w', 'delta_ssm_conv_b', 'delta_ssm_a_log', 'delta_ssm_dt_bias', 'delta_ssm_d', 'delta_ssm_norm_g', 'delta_w_branch_a', 'delta_w_branch_b', 'delta_w_branch_c', 'delta_w_out', 'delta_final_norm_g', 'new_m_meta_tokens', 'new_m_norm_g', 'new_m_w_in', 'new_m_gdn_conv_w', 'new_m_gdn_a_log', 'new_m_gdn_dt_bias', 'new_m_gdn_norm_g', 'new_m_ssm_conv_w', 'new_m_ssm_conv_b', 'new_m_ssm_a_log', 'new_m_ssm_dt_bias', 'new_m_ssm_d', 'new_m_ssm_norm_g', 'new_m_w_branch_a', 'new_m_w_branch_b', 'new_m_w_branch_c', 'new_m_w_out', 'new_m_final_norm_g', 'new_v_meta_tokens', 'new_v_norm_g', 'new_v_w_in', 'new_v_gdn_conv_w', 'new_v_gdn_a_log', 'new_v_gdn_dt_bias', 'new_v_gdn_norm_g', 'new_v_ssm_conv_w', 'new_v_ssm_conv_b', 'new_v_ssm_a_log', 'new_v_ssm_dt_bias', 'new_v_ssm_d', 'new_v_ssm_norm_g', 'new_v_w_branch_a', 'new_v_w_branch_b', 'new_v_w_branch_c', 'new_v_w_out', 'new_v_final_norm_g']
TWIN_LEAF_KINDS = {'loss': 'loss', 'grad_x': 'grad_x', 'grad_meta_tokens': 'grad_w', 'grad_norm_g': 'grad_w', 'grad_w_in': 'grad_w', 'grad_gdn_conv_w': 'grad_w', 'grad_gdn_a_log': 'grad_w', 'grad_gdn_dt_bias': 'grad_w', 'grad_gdn_norm_g': 'grad_w', 'grad_ssm_conv_w': 'grad_w', 'grad_ssm_conv_b': 'grad_w', 'grad_ssm_a_log': 'grad_w', 'grad_ssm_dt_bias': 'grad_w', 'grad_ssm_d': 'grad_w', 'grad_ssm_norm_g': 'grad_w', 'grad_w_branch_a': 'grad_w', 'grad_w_branch_b': 'grad_w', 'grad_w_branch_c': 'grad_w', 'grad_w_out': 'grad_w', 'grad_final_norm_g': 'grad_w', 'delta_meta_tokens': 'delta_w', 'delta_norm_g': 'delta_w', 'delta_w_in': 'delta_w', 'delta_gdn_conv_w': 'delta_w', 'delta_gdn_a_log': 'delta_w', 'delta_gdn_dt_bias': 'delta_w', 'delta_gdn_norm_g': 'delta_w', 'delta_ssm_conv_w': 'delta_w', 'delta_ssm_conv_b': 'delta_w', 'delta_ssm_a_log': 'delta_w', 'delta_ssm_dt_bias': 'delta_w', 'delta_ssm_d': 'delta_w', 'delta_ssm_norm_g': 'delta_w', 'delta_w_branch_a': 'delta_w', 'delta_w_branch_b': 'delta_w', 'delta_w_branch_c': 'delta_w', 'delta_w_out': 'delta_w', 'delta_final_norm_g': 'delta_w', 'new_m_meta_tokens': 'new_m', 'new_m_norm_g': 'new_m', 'new_m_w_in': 'new_m', 'new_m_gdn_conv_w': 'new_m', 'new_m_gdn_a_log': 'new_m', 'new_m_gdn_dt_bias': 'new_m', 'new_m_gdn_norm_g': 'new_m', 'new_m_ssm_conv_w': 'new_m', 'new_m_ssm_conv_b': 'new_m', 'new_m_ssm_a_log': 'new_m', 'new_m_ssm_dt_bias': 'new_m', 'new_m_ssm_d': 'new_m', 'new_m_ssm_norm_g': 'new_m', 'new_m_w_branch_a': 'new_m', 'new_m_w_branch_b': 'new_m', 'new_m_w_branch_c': 'new_m', 'new_m_w_out': 'new_m', 'new_m_final_norm_g': 'new_m', 'new_v_meta_tokens': 'new_v', 'new_v_norm_g': 'new_v', 'new_v_w_in': 'new_v', 'new_v_gdn_conv_w': 'new_v', 'new_v_gdn_a_log': 'new_v', 'new_v_gdn_dt_bias': 'new_v', 'new_v_gdn_norm_g': 'new_v', 'new_v_ssm_conv_w': 'new_v', 'new_v_ssm_conv_b': 'new_v', 'new_v_ssm_a_log': 'new_v', 'new_v_ssm_dt_bias': 'new_v', 'new_v_ssm_d': 'new_v', 'new_v_ssm_norm_g': 'new_v', 'new_v_w_branch_a': 'new_v', 'new_v_w_branch_b': 'new_v', 'new_v_w_branch_c': 'new_v', 'new_v_w_out': 'new_v', 'new_v_final_norm_g': 'new_v'}


def _forward(args):
    return _fwd_reference(*[args[k] for k in FWD_PARAMS])


def _output_shape():
    out = _jax.eval_shape(lambda: _forward(_fwd_setup_inputs(0)))
    return out.shape, out.dtype

N_MICROBATCH = 1
ADAM_LR = 0.001
ADAM_B1 = 0.9
ADAM_B2 = 0.999
ADAM_EPS = 1e-08
ADAM_WD = 0.01
ADAM_STEP = 10
PER_EXAMPLE_BATCH_AXIS = {'x': 0, 'loss_target': 0}
SHARED_INPUTS = []
_WEIGHT_DTYPES = {'meta_tokens': _jnp.float32, 'norm_g': _jnp.float32, 'w_in': _jnp.float32, 'gdn_conv_w': _jnp.float32, 'gdn_a_log': _jnp.float32, 'gdn_dt_bias': _jnp.float32, 'gdn_norm_g': _jnp.float32, 'ssm_conv_w': _jnp.float32, 'ssm_conv_b': _jnp.float32, 'ssm_a_log': _jnp.float32, 'ssm_dt_bias': _jnp.float32, 'ssm_d': _jnp.float32, 'ssm_norm_g': _jnp.float32, 'w_branch_a': _jnp.float32, 'w_branch_b': _jnp.float32, 'w_branch_c': _jnp.float32, 'w_out': _jnp.float32, 'final_norm_g': _jnp.float32}
MOMENT_SCALE = {'meta_tokens': 6.115580e-03, 'norm_g': 1.202126e-01, 'w_in': 3.033406e-02, 'gdn_conv_w': 2.764757e-02, 'gdn_a_log': 1.914253e-01, 'gdn_dt_bias': 1.856540e-01, 'gdn_norm_g': 9.841715e-02, 'ssm_conv_w': 4.144690e-02, 'ssm_conv_b': 5.982488e-02, 'ssm_a_log': 1.669362e-01, 'ssm_dt_bias': 9.356388e-02, 'ssm_d': 2.422391e-01, 'ssm_norm_g': 4.484810e-02, 'w_branch_a': 2.377536e-02, 'w_branch_b': 3.555001e-02, 'w_branch_c': 6.165896e-02, 'w_out': 7.513461e-02, 'final_norm_g': 1.604445e+01}


def _to_microbatches(a, axis):
    t = _jnp.moveaxis(a, axis, 0)
    t = t.reshape((N_MICROBATCH, t.shape[0] // N_MICROBATCH) + t.shape[1:])
    return _jnp.moveaxis(t, 1, axis + 1)


def setup_inputs(seed: int = 0) -> dict:
    inp = _fwd_setup_inputs(seed)
    key = _jax.random.fold_in(_jax.random.key(seed), 7919)
    shape, _ = _output_shape()
    out = dict(inp)
    out["loss_target"] = _jax.random.normal(_jax.random.fold_in(key, 0), shape, _jnp.float32)
    for i, name in enumerate(TWIN_WEIGHTS):
        w = inp[name].astype(_jnp.float32)
        if MOMENT_SCALE is None:
            s = _jnp.sqrt(_jnp.mean(_jnp.square(w)) + 1e-30)
        else:
            s = MOMENT_SCALE[name]
        km, kv = _jax.random.split(_jax.random.fold_in(key, i + 1))
        out[name] = w
        out["m_" + name] = s * _jax.random.normal(km, w.shape, _jnp.float32)
        out["v_" + name] = (s * s) * _jax.random.uniform(kv, w.shape, _jnp.float32, 0.5, 1.5)
    if N_MICROBATCH > 1:
        for name, axis in PER_EXAMPLE_BATCH_AXIS.items():
            out[name] = _to_microbatches(out[name], axis)
    return {'x': out['x'], 'meta_tokens': out['meta_tokens'], 'norm_g': out['norm_g'], 'w_in': out['w_in'], 'gdn_conv_w': out['gdn_conv_w'], 'gdn_a_log': out['gdn_a_log'], 'gdn_dt_bias': out['gdn_dt_bias'], 'gdn_norm_g': out['gdn_norm_g'], 'ssm_conv_w': out['ssm_conv_w'], 'ssm_conv_b': out['ssm_conv_b'], 'ssm_a_log': out['ssm_a_log'], 'ssm_dt_bias': out['ssm_dt_bias'], 'ssm_d': out['ssm_d'], 'ssm_norm_g': out['ssm_norm_g'], 'w_branch_a': out['w_branch_a'], 'w_branch_b': out['w_branch_b'], 'w_branch_c': out['w_branch_c'], 'w_out': out['w_out'], 'final_norm_g': out['final_norm_g'], 'loss_target': out['loss_target'], 'm_meta_tokens': out['m_meta_tokens'], 'm_norm_g': out['m_norm_g'], 'm_w_in': out['m_w_in'], 'm_gdn_conv_w': out['m_gdn_conv_w'], 'm_gdn_a_log': out['m_gdn_a_log'], 'm_gdn_dt_bias': out['m_gdn_dt_bias'], 'm_gdn_norm_g': out['m_gdn_norm_g'], 'm_ssm_conv_w': out['m_ssm_conv_w'], 'm_ssm_conv_b': out['m_ssm_conv_b'], 'm_ssm_a_log': out['m_ssm_a_log'], 'm_ssm_dt_bias': out['m_ssm_dt_bias'], 'm_ssm_d': out['m_ssm_d'], 'm_ssm_norm_g': out['m_ssm_norm_g'], 'm_w_branch_a': out['m_w_branch_a'], 'm_w_branch_b': out['m_w_branch_b'], 'm_w_branch_c': out['m_w_branch_c'], 'm_w_out': out['m_w_out'], 'm_final_norm_g': out['m_final_norm_g'], 'v_meta_tokens': out['v_meta_tokens'], 'v_norm_g': out['v_norm_g'], 'v_w_in': out['v_w_in'], 'v_gdn_conv_w': out['v_gdn_conv_w'], 'v_gdn_a_log': out['v_gdn_a_log'], 'v_gdn_dt_bias': out['v_gdn_dt_bias'], 'v_gdn_norm_g': out['v_gdn_norm_g'], 'v_ssm_conv_w': out['v_ssm_conv_w'], 'v_ssm_conv_b': out['v_ssm_conv_b'], 'v_ssm_a_log': out['v_ssm_a_log'], 'v_ssm_dt_bias': out['v_ssm_dt_bias'], 'v_ssm_d': out['v_ssm_d'], 'v_ssm_norm_g': out['v_ssm_norm_g'], 'v_w_branch_a': out['v_w_branch_a'], 'v_w_branch_b': out['v_w_branch_b'], 'v_w_branch_c': out['v_w_branch_c'], 'v_w_out': out['v_w_out'], 'v_final_norm_g': out['v_final_norm_g']}


def _loss(weights, diff, rest, loss_target):
    with _jax.named_scope("forward"):
        args = {**rest, TWIN_DIFF_INPUT: diff, **{k: w.astype(_WEIGHT_DTYPES[k]) for k, w in weights.items()}}
        y = _forward(args)
    with _jax.named_scope("loss_head"):
        err = _jnp.square(y.astype(_jnp.float32) - loss_target)
        return 0.5 * _jnp.sum(_jnp.mean(err, axis=-1)) if err.ndim else 0.5 * err


def _adamw(w, g, m, v):
    m = ADAM_B1 * m + (1.0 - ADAM_B1) * g
    v = ADAM_B2 * v + (1.0 - ADAM_B2) * _jnp.square(g)
    m_hat = m / (1.0 - ADAM_B1 ** ADAM_STEP)
    v_hat = v / (1.0 - ADAM_B2 ** ADAM_STEP)
    delta = -ADAM_LR * (m_hat / (_jnp.sqrt(v_hat) + ADAM_EPS) + ADAM_WD * w)
    return delta, m, v


def reference(x, meta_tokens, norm_g, w_in, gdn_conv_w, gdn_a_log, gdn_dt_bias, gdn_norm_g, ssm_conv_w, ssm_conv_b, ssm_a_log, ssm_dt_bias, ssm_d, ssm_norm_g, w_branch_a, w_branch_b, w_branch_c, w_out, final_norm_g, loss_target, m_meta_tokens, m_norm_g, m_w_in, m_gdn_conv_w, m_gdn_a_log, m_gdn_dt_bias, m_gdn_norm_g, m_ssm_conv_w, m_ssm_conv_b, m_ssm_a_log, m_ssm_dt_bias, m_ssm_d, m_ssm_norm_g, m_w_branch_a, m_w_branch_b, m_w_branch_c, m_w_out, m_final_norm_g, v_meta_tokens, v_norm_g, v_w_in, v_gdn_conv_w, v_gdn_a_log, v_gdn_dt_bias, v_gdn_norm_g, v_ssm_conv_w, v_ssm_conv_b, v_ssm_a_log, v_ssm_dt_bias, v_ssm_d, v_ssm_norm_g, v_w_branch_a, v_w_branch_b, v_w_branch_c, v_w_out, v_final_norm_g):
    given = dict(x=x, meta_tokens=meta_tokens, norm_g=norm_g, w_in=w_in, gdn_conv_w=gdn_conv_w, gdn_a_log=gdn_a_log, gdn_dt_bias=gdn_dt_bias, gdn_norm_g=gdn_norm_g, ssm_conv_w=ssm_conv_w, ssm_conv_b=ssm_conv_b, ssm_a_log=ssm_a_log, ssm_dt_bias=ssm_dt_bias, ssm_d=ssm_d, ssm_norm_g=ssm_norm_g, w_branch_a=w_branch_a, w_branch_b=w_branch_b, w_branch_c=w_branch_c, w_out=w_out, final_norm_g=final_norm_g, loss_target=loss_target, m_meta_tokens=m_meta_tokens, m_norm_g=m_norm_g, m_w_in=m_w_in, m_gdn_conv_w=m_gdn_conv_w, m_gdn_a_log=m_gdn_a_log, m_gdn_dt_bias=m_gdn_dt_bias, m_gdn_norm_g=m_gdn_norm_g, m_ssm_conv_w=m_ssm_conv_w, m_ssm_conv_b=m_ssm_conv_b, m_ssm_a_log=m_ssm_a_log, m_ssm_dt_bias=m_ssm_dt_bias, m_ssm_d=m_ssm_d, m_ssm_norm_g=m_ssm_norm_g, m_w_branch_a=m_w_branch_a, m_w_branch_b=m_w_branch_b, m_w_branch_c=m_w_branch_c, m_w_out=m_w_out, m_final_norm_g=m_final_norm_g, v_meta_tokens=v_meta_tokens, v_norm_g=v_norm_g, v_w_in=v_w_in, v_gdn_conv_w=v_gdn_conv_w, v_gdn_a_log=v_gdn_a_log, v_gdn_dt_bias=v_gdn_dt_bias, v_gdn_norm_g=v_gdn_norm_g, v_ssm_conv_w=v_ssm_conv_w, v_ssm_conv_b=v_ssm_conv_b, v_ssm_a_log=v_ssm_a_log, v_ssm_dt_bias=v_ssm_dt_bias, v_ssm_d=v_ssm_d, v_ssm_norm_g=v_ssm_norm_g, v_w_branch_a=v_w_branch_a, v_w_branch_b=v_w_branch_b, v_w_branch_c=v_w_branch_c, v_w_out=v_w_out, v_final_norm_g=v_final_norm_g)
    weights = {n: given[n] for n in TWIN_WEIGHTS}
    shared = {n: given[n] for n in SHARED_INPUTS}
    per_example = {n: given[n] for n in ['x']}
    grad_fn = _jax.value_and_grad(_loss, argnums=(0, 1))

    def one_microbatch(ex, loss_target):
        ex = dict(ex)
        diff = ex.pop(TWIN_DIFF_INPUT)
        return grad_fn(weights, diff, {**shared, **ex}, loss_target)

    if N_MICROBATCH == 1:
        loss, (grad_w, grad_x) = one_microbatch(per_example, given["loss_target"])
    else:
        def body(carry, xs):
            loss_sum, grad_sum = carry
            l_k, (gw_k, gx_k) = one_microbatch(xs[0], xs[1])
            with _jax.named_scope("update"):
                return (loss_sum + l_k, _jax.tree.map(_jnp.add, grad_sum, gw_k)), gx_k

        init = (_jnp.zeros((), _jnp.float32), _jax.tree.map(_jnp.zeros_like, weights))
        (loss, grad_w), grad_x = _jax.lax.scan(body, init, (per_example, given["loss_target"]))
    with _jax.named_scope("update"):
        delta_w, new_m, new_v = {}, {}, {}
        for n in TWIN_WEIGHTS:
            delta_w[n], new_m[n], new_v[n] = _adamw(weights[n], grad_w[n], given["m_" + n], given["v_" + n])
    return (loss, grad_x, *[grad_w[n] for n in TWIN_WEIGHTS], *[delta_w[n] for n in TWIN_WEIGHTS],
            *[new_m[n] for n in TWIN_WEIGHTS], *[new_v[n] for n in TWIN_WEIGHTS])
```

```python
import functools
import math

import jax
import jax.numpy as jnp
from jax import lax
from jax.experimental import pallas as pl
from jax.experimental.pallas import tpu as pltpu

F32 = jnp.float32
BF16 = jnp.bfloat16

N_DEV = 8
N_META = 16
D_MODEL = 1024
DEPTH = 4
RMS_EPS = 1e-6
L2_EPS = 1e-6
CONV_K = 4
HEAD = 128
N_HEADS = 8
CHUNK = 64
SSM_INNER = 2048
SSM_P = 64
SSM_HEADS = 32
SSM_N = 128
SSM_GROUPS = 2
GDN_CONV_CH = 3072
SSM_CONV_CH = 2560
LANE = 128
VMEM_LIMIT = 56 * 1024 * 1024

ADAM_LR = 0.001
ADAM_B1 = 0.9
ADAM_B2 = 0.999
ADAM_EPS = 1e-08
ADAM_WD = 0.01
ADAM_STEP = 10

D_IN = 15920
D_INC = 16000
C_SBQ, C_SBK, C_SBV, C_SBZ = 0, 1024, 2048, 3072
C_GDQ, C_GDK, C_GDV, C_GDZ = 4096, 5120, 6144, 7168
C_SSZ = 8192
C_GATE = 10240
C_XBC = 13312
C_MISC = 15872
_SEGS = (
    (0, 8192, 0),
    (8192, 16, C_MISC),
    (8208, 2048, C_SSZ),
    (10256, 2560, C_XBC),
    (12816, 32, C_MISC + 16),
    (12848, 3072, C_GATE),
)


def _cparams(**kw):
    return pltpu.CompilerParams(vmem_limit_bytes=VMEM_LIMIT, **kw)


def _bf(x):
    return x.astype(BF16)


def _dg(a, b, ca, cb):
    return lax.dot_general(a, b, (((ca,), (cb,)), ((), ())), preferred_element_type=F32)


@jax.custom_vjp
def mm_nn(a, b):
    return _dg(_bf(a), _bf(b), 1, 0)


def _mm_nn_fwd(a, b):
    a, b = _bf(a), _bf(b)
    return _dg(a, b, 1, 0), (a, b)


def _mm_nn_bwd(res, g):
    a, b = res
    g = _bf(g)
    return _dg(g, b, 1, 1), _dg(a, g, 0, 0)


mm_nn.defvjp(_mm_nn_fwd, _mm_nn_bwd)


@jax.custom_vjp
def mm_nt(a, b):
    return _dg(_bf(a), _bf(b), 1, 1)


def _mm_nt_fwd(a, b):
    a, b = _bf(a), _bf(b)
    return _dg(a, b, 1, 1), (a, b)


def _mm_nt_bwd(res, g):
    a, b = res
    g = _bf(g)
    return _dg(g, b, 1, 0), _dg(g, a, 0, 0)


mm_nt.defvjp(_mm_nt_fwd, _mm_nt_bwd)


@jax.custom_vjp
def mm_tn(a, b):
    return _dg(_bf(a), _bf(b), 0, 0)


def _mm_tn_fwd(a, b):
    a, b = _bf(a), _bf(b)
    return _dg(a, b, 0, 0), (a, b)


def _mm_tn_bwd(res, g):
    a, b = res
    g = _bf(g)
    return _dg(b, g, 1, 1), _dg(a, g, 1, 0)


mm_tn.defvjp(_mm_tn_fwd, _mm_tn_bwd)


def _split2(x):
    hi = _bf(x)
    return hi, _bf(x - hi.astype(F32))


def _split3(x):
    hi = _bf(x)
    r = x - hi.astype(F32)
    mid = _bf(r)
    return hi, mid, _bf(r - mid.astype(F32))


def _dg3(a, b, ca, cb):
    ah, al = _split2(a)
    bh, bl = _split2(b)
    return _dg(ah, bh, ca, cb) + (_dg(ah, bl, ca, cb) + _dg(al, bh, ca, cb))


def _iota(shape, dim):
    return lax.broadcasted_iota(jnp.int32, shape, dim)


def _const_left(c, x, ca):
    hi, mid, lo = _split3(x)
    return _dg(c, hi, ca, 0) + (_dg(c, mid, ca, 0) + _dg(c, lo, ca, 0))


def _tril_incl(n):
    return (_iota((n, n), 0) >= _iota((n, n), 1)).astype(BF16)


@jax.custom_vjp
def cumsum_rows(x):
    return _const_left(_tril_incl(x.shape[0]), x, 1)


def _cumsum_rows_fwd(x):
    return cumsum_rows(x), None


def _cumsum_rows_bwd(_, g):
    return (_const_left(_tril_incl(g.shape[0]), g, 0),)


cumsum_rows.defvjp(_cumsum_rows_fwd, _cumsum_rows_bwd)


def _expand_mat(n_in, first, group, n_out):
    return (_iota((n_in, n_out), 0) == first + _iota((n_in, n_out), 1) // group).astype(BF16)


def _right_const(x, c, cc):
    hi, mid, lo = _split3(x)
    return _dg(hi, c, 1, cc) + (_dg(mid, c, 1, cc) + _dg(lo, c, 1, cc))


def make_expand(first, group, n_out):
    @jax.custom_vjp
    def expand(x):
        return _right_const(x, _expand_mat(x.shape[1], first, group, n_out), 0)

    def fwd(x):
        return expand(x), None

    def bwd(_, g):
        return (_right_const(g, _expand_mat(LANE, first, group, n_out), 1),)

    expand.defvjp(fwd, bwd)
    return expand


def _tri_inv_impl(m):
    n = m.shape[0]
    r, c = _iota((n, n), 0), _iota((n, n), 1)
    eye = (r == c).astype(F32)
    d = jnp.where(r // 8 == c // 8, m, 0.0)
    t = eye - d
    p = _dg3(d, d, 1, 0)
    t = t + _dg3(t, p, 1, 0)
    p = _dg3(p, p, 1, 0)
    t = t + _dg3(t, p, 1, 0)
    for blk in (16, 32, 64):
        off = jnp.where((r // blk == c // blk) & (r // (blk // 2) != c // (blk // 2)), m, 0.0)
        t = t - _dg3(_dg3(t, off, 1, 0), t, 1, 0)
    return t


@jax.custom_vjp
def tri_inv(m):
    return _tri_inv_impl(m)


def _tri_inv_fwd(m):
    t = _tri_inv_impl(m)
    return t, t


def _tri_inv_bwd(t, g):
    return (-_dg3(_dg3(t, g, 0, 0), t, 1, 1),)


tri_inv.defvjp(_tri_inv_fwd, _tri_inv_bwd)


def _sigmoid(x):
    return 1.0 / (1.0 + jnp.exp(-x))


def _silu(x):
    return x * _sigmoid(x)


def _softplus(x):
    return jnp.maximum(x, 0.0) + jnp.log(1.0 + jnp.exp(-jnp.abs(x)))


def _row_of_diag(cb):
    n = cb.shape[0]
    return jnp.sum(jnp.where(_iota((n, n), 0) == _iota((n, n), 1), cb, 0.0), axis=0, keepdims=True)


def gdn_chunk(q, k, v, bx, gx, s):
    n = q.shape[0]
    gc = cumsum_rows(gx)
    gl = jnp.sum(gx, axis=0, keepdims=True)
    cb = gc[:, :n]
    seg = cb - _row_of_diag(cb)
    r, c = _iota((n, n), 0), _iota((n, n), 1)
    dec_strict = jnp.exp(jnp.where(r > c, seg, -1e30))
    dec_incl = jnp.exp(jnp.where(r >= c, seg, -1e30))
    kb = k * bx
    t = tri_inv(mm_nt(kb, k) * dec_strict)
    egc = jnp.exp(gc)
    u = mm_nn(t, v * bx)
    w = mm_nn(t, kb * egc)
    aqk = mm_nt(q, k) * dec_incl
    v_new = u - mm_nn(w, s)
    o = mm_nn(q * egc, s) + mm_nn(aqk, v_new)
    s_new = s * jnp.exp(gl) + mm_tn(k * jnp.exp(gl - gc), v_new)
    return o, s_new


def ssd_chunk(xs, la, b, c, st):
    n = xs.shape[0]
    cs = cumsum_rows(la)
    cl = jnp.sum(la, axis=0, keepdims=True)
    seg = cs - _row_of_diag(cs)
    r, cc = _iota((n, n), 0), _iota((n, n), 1)
    dec = jnp.exp(jnp.where(r >= cc, seg, -1e30))
    a = mm_nt(c, b) * dec
    y = mm_nn(a, xs) + mm_nn(c, st) * jnp.exp(cs)
    st_new = st * jnp.exp(cl) + mm_tn(b, xs * jnp.exp(cl - cs))
    return y, st_new


def _sds(shape, dtype=F32):
    return jax.ShapeDtypeStruct(shape, dtype)


_GDN_V_BLK = 2 * N_HEADS


def gdn_scan_fwd(q, k, v, bx, gx):
    lp = q.shape[0]
    nc = lp // CHUNK

    def body(q_ref, k_ref, v_ref, b_ref, g_ref, o_ref, st_ref, s_scr):
        h = pl.program_id(1)

        @pl.when(pl.program_id(0) == 0)
        def _():
            s_scr[h] = jnp.zeros((HEAD, HEAD), F32)

        s = s_scr[h]
        st_ref[0, 0] = s
        o, s_new = gdn_chunk(q_ref[...], k_ref[...], v_ref[...], b_ref[...], g_ref[...], s)
        o_ref[...] = o
        s_scr[h] = s_new

    blk = pl.BlockSpec((CHUNK, HEAD), lambda c, h: (c, h))
    return pl.pallas_call(
        body, name="gdn_scan_fwd", grid=(nc, N_HEADS),
        in_specs=[blk, blk, pl.BlockSpec((CHUNK, HEAD), lambda c, h: (c, _GDN_V_BLK + h)), blk, blk],
        out_specs=[blk, pl.BlockSpec((1, 1, HEAD, HEAD), lambda c, h: (c, h, 0, 0))],
        out_shape=[_sds((lp, N_HEADS * HEAD)), _sds((nc, N_HEADS, HEAD, HEAD))],
        scratch_shapes=[pltpu.VMEM((N_HEADS, HEAD, HEAD), F32)],
        compiler_params=_cparams(dimension_semantics=("arbitrary", "arbitrary")),
    )(q, k, v, bx, gx)


def gdn_scan_bwd(q, k, v, bx, gx, st, do):
    lp = q.shape[0]
    nc = lp // CHUNK

    def body(q_ref, k_ref, v_ref, b_ref, g_ref, st_ref, do_ref, dq_ref, dk_ref, dv_ref, db_ref, dg_ref, ds_scr):
        h = pl.program_id(1)

        @pl.when(pl.program_id(0) == 0)
        def _():
            ds_scr[h] = jnp.zeros((HEAD, HEAD), F32)

        _, vjp = jax.vjp(gdn_chunk, q_ref[...], k_ref[...], v_ref[...], b_ref[...], g_ref[...], st_ref[0, 0])
        dq, dk, dv, db, dg, ds = vjp((do_ref[...], ds_scr[h]))
        dq_ref[...] = dq
        dk_ref[...] = dk
        dv_ref[...] = dv
        db_ref[...] = db
        dg_ref[...] = dg
        ds_scr[h] = ds

    blk = pl.BlockSpec((CHUNK, HEAD), lambda c, h: (nc - 1 - c, h))
    return pl.pallas_call(
        body, name="gdn_scan_bwd", grid=(nc, N_HEADS),
        in_specs=[blk, blk, pl.BlockSpec((CHUNK, HEAD), lambda c, h: (nc - 1 - c, _GDN_V_BLK + h)), blk, blk,
                  pl.BlockSpec((1, 1, HEAD, HEAD), lambda c, h: (nc - 1 - c, h, 0, 0)), blk],
        out_specs=[blk] * 5,
        out_shape=[_sds((lp, N_HEADS * HEAD))] * 5,
        scratch_shapes=[pltpu.VMEM((N_HEADS, HEAD, HEAD), F32)],
        compiler_params=_cparams(dimension_semantics=("arbitrary", "arbitrary")),
    )(q, k, v, bx, gx, st, do)


_SSD_PAIRS = SSM_HEADS // 2
_XBC_B_BLK = SSM_INNER // LANE
_XBC_C_BLK = _XBC_B_BLK + SSM_GROUPS


def ssd_scan_fwd(xs, la, xbc):
    lp = xs.shape[0]
    nc = lp // CHUNK

    def body(xs_ref, la_ref, b_ref, c_ref, y_ref, st_ref, s_scr):
        j = pl.program_id(1)

        @pl.when(pl.program_id(0) == 0)
        def _():
            s_scr[pl.ds(2 * j, 2)] = jnp.zeros((2, SSM_N, SSM_P), F32)

        b, c = b_ref[...], c_ref[...]
        for hh in range(2):
            lanes = slice(hh * SSM_P, (hh + 1) * SSM_P)
            s = s_scr[2 * j + hh]
            st_ref[0, hh] = s
            y, s_new = ssd_chunk(xs_ref[:, lanes], la_ref[:, lanes], b, c, s)
            y_ref[:, lanes] = y
            s_scr[2 * j + hh] = s_new

    blk = pl.BlockSpec((CHUNK, LANE), lambda c, j: (c, j))
    per_group = _SSD_PAIRS // SSM_GROUPS
    return pl.pallas_call(
        body, name="ssd_scan_fwd", grid=(nc, _SSD_PAIRS),
        in_specs=[blk, blk,
                  pl.BlockSpec((CHUNK, LANE), lambda c, j: (c, _XBC_B_BLK + j // per_group)),
                  pl.BlockSpec((CHUNK, LANE), lambda c, j: (c, _XBC_C_BLK + j // per_group))],
        out_specs=[blk, pl.BlockSpec((1, 2, SSM_N, SSM_P), lambda c, j: (c, j, 0, 0))],
        out_shape=[_sds((lp, SSM_INNER)), _sds((nc, SSM_HEADS, SSM_N, SSM_P))],
        scratch_shapes=[pltpu.VMEM((SSM_HEADS, SSM_N, SSM_P), F32)],
        compiler_params=_cparams(dimension_semantics=("arbitrary", "arbitrary")),
    )(xs, la, xbc, xbc)


def ssd_scan_bwd(xs, la, xbc, st, dy):
    lp = xs.shape[0]
    nc = lp // CHUNK
    per_group = _SSD_PAIRS // SSM_GROUPS

    def body(xs_ref, la_ref, b_ref, c_ref, st_ref, dy_ref, dxs_ref, dla_ref, db_ref, dc_ref, ds_scr):
        j = pl.program_id(1)

        @pl.when(pl.program_id(0) == 0)
        def _():
            ds_scr[pl.ds(2 * j, 2)] = jnp.zeros((2, SSM_N, SSM_P), F32)

        @pl.when(j % per_group == 0)
        def _():
            db_ref[...] = jnp.zeros((CHUNK, LANE), F32)
            dc_ref[...] = jnp.zeros((CHUNK, LANE), F32)

        b, c = b_ref[...], c_ref[...]
        for hh in range(2):
            lanes = slice(hh * SSM_P, (hh + 1) * SSM_P)
            _, vjp = jax.vjp(ssd_chunk, xs_ref[:, lanes], la_ref[:, lanes], b, c, st_ref[0, hh])
            dxs, dla, db, dc, ds = vjp((dy_ref[:, lanes], ds_scr[2 * j + hh]))
            dxs_ref[:, lanes] = dxs
            dla_ref[:, lanes] = dla
            db_ref[...] += db
            dc_ref[...] += dc
            ds_scr[2 * j + hh] = ds

    blk = pl.BlockSpec((CHUNK, LANE), lambda c, j: (nc - 1 - c, j))
    return pl.pallas_call(
        body, name="ssd_scan_bwd", grid=(nc, _SSD_PAIRS),
        in_specs=[blk, blk,
                  pl.BlockSpec((CHUNK, LANE), lambda c, j: (nc - 1 - c, _XBC_B_BLK + j // per_group)),
                  pl.BlockSpec((CHUNK, LANE), lambda c, j: (nc - 1 - c, _XBC_C_BLK + j // per_group)),
                  pl.BlockSpec((1, 2, SSM_N, SSM_P), lambda c, j: (nc - 1 - c, j, 0, 0)), blk],
        out_specs=[blk, blk,
                   pl.BlockSpec((CHUNK, LANE), lambda c, j: (nc - 1 - c, j // per_group)),
                   pl.BlockSpec((CHUNK, LANE), lambda c, j: (nc - 1 - c, j // per_group))],
        out_shape=[_sds((lp, SSM_INNER)), _sds((lp, SSM_INNER)),
                   _sds((lp, SSM_GROUPS * SSM_N)), _sds((lp, SSM_GROUPS * SSM_N))],
        scratch_shapes=[pltpu.VMEM((SSM_HEADS, SSM_N, SSM_P), F32)],
        compiler_params=_cparams(dimension_semantics=("arbitrary", "arbitrary")),
    )(xs, la, xbc, xbc, st, dy)


SB_BLK = 128
_SB_SCALE = HEAD ** -0.5


def _mm2(x, c):
    hi, lo = _split2(x)
    return _dg(hi, c, 1, 0) + _dg(lo, c, 1, 0)


def _sb_tile(q, kj, diag):
    z = _dg(q, _bf(kj), 1, 1) * _SB_SCALE
    ls = -_softplus(-z)
    r, c = _iota((SB_BLK, SB_BLK), 0), _iota((SB_BLK, SB_BLK), 1)
    valid = jnp.logical_or(jnp.logical_not(diag), c < r)
    lk = ls - z
    return ls, lk, jnp.where(valid, lk, 0.0), valid


def sb_attn_fwd(proj):
    lp = proj.shape[0]
    nq = lp // SB_BLK

    def body(q_ref, k_ref, v_ref, o_ref, tot_ref):
        i = pl.program_id(1)
        q = _bf(q_ref[...])
        r, c = _iota((SB_BLK, SB_BLK), 0), _iota((SB_BLK, SB_BLK), 1)
        later_mat = (r > c).astype(BF16)

        def step(t, carry):
            cs, acc = carry
            j = i - t
            rows = pl.ds(pl.multiple_of(j * SB_BLK, SB_BLK), SB_BLK)
            ls, _, lkm, valid = _sb_tile(q, k_ref[rows, :], t == 0)
            w = jnp.where(valid, jnp.exp(ls + _mm2(lkm, later_mat) + cs), 0.0)
            acc = acc + _dg(_bf(w), _bf(v_ref[rows, :]), 1, 0)
            return cs + jnp.sum(lkm, axis=1, keepdims=True), acc

        cs, acc = lax.fori_loop(0, i + 1, step, (jnp.zeros((SB_BLK, 1), F32), jnp.zeros((SB_BLK, HEAD), F32)))
        o_ref[...] = acc
        tot_ref[...] = jnp.broadcast_to(cs, (SB_BLK, HEAD))

    qblk = C_SBQ // HEAD
    kblk = C_SBK // HEAD
    vblk = C_SBV // HEAD
    out_blk = pl.BlockSpec((SB_BLK, HEAD), lambda h, i: (i, h))
    return pl.pallas_call(
        body, name="sb_attn_fwd", grid=(N_HEADS, nq),
        in_specs=[pl.BlockSpec((SB_BLK, HEAD), lambda h, i: (i, qblk + h)),
                  pl.BlockSpec((lp, HEAD), lambda h, i: (0, kblk + h)),
                  pl.BlockSpec((lp, HEAD), lambda h, i: (0, vblk + h))],
        out_specs=[out_blk, out_blk],
        out_shape=[_sds((lp, N_HEADS * HEAD)), _sds((lp, N_HEADS * HEAD))],
        compiler_params=_cparams(dimension_semantics=("arbitrary", "arbitrary")),
    )(proj, proj, proj)


def sb_attn_bwd(proj, tot, do):
    lp = proj.shape[0]
    nq = lp // SB_BLK

    def body(q_ref, k_ref, v_ref, tot_ref, do_ref, dq_ref, dk_ref, dv_ref):
        i = pl.program_id(1)

        @pl.when(i == 0)
        def _():
            dk_ref[...] = jnp.zeros((lp, HEAD), F32)
            dv_ref[...] = jnp.zeros((lp, HEAD), F32)

        q = _bf(q_ref[...])
        do = _bf(do_ref[...])
        tot = tot_ref[...]
        r, c = _iota((SB_BLK, SB_BLK), 0), _iota((SB_BLK, SB_BLK), 1)
        upto_mat = (r <= c).astype(BF16)
        before_mat = (r < c).astype(BF16)

        def step(j, carry):
            pre, gs, dq = carry
            rows = pl.ds(pl.multiple_of(j * SB_BLK, SB_BLK), SB_BLK)
            kj = _bf(k_ref[rows, :])
            vj = _bf(v_ref[rows, :])
            ls, lk, lkm, valid = _sb_tile(q, kj, j == i)
            w = jnp.where(valid, jnp.exp(ls + (tot - (pre + _mm2(lkm, upto_mat)))), 0.0)
            g = w * _dg(do, vj, 1, 1)
            dlk = gs + _mm2(g, before_mat)
            dz = _bf(jnp.where(valid, g * jnp.exp(lk) - dlk * jnp.exp(ls), 0.0) * _SB_SCALE)
            dk_ref[rows, :] += _dg(dz, q, 0, 0)
            dv_ref[rows, :] += _dg(_bf(w), do, 0, 0)
            return (pre + jnp.sum(lkm, axis=1, keepdims=True), gs + jnp.sum(g, axis=1, keepdims=True),
                    dq + _dg(dz, kj, 1, 0))

        zero = jnp.zeros((SB_BLK, 1), F32)
        _, _, dq = lax.fori_loop(0, i + 1, step, (zero, zero, jnp.zeros((SB_BLK, HEAD), F32)))
        dq_ref[...] = dq

    qblk = C_SBQ // HEAD
    kblk = C_SBK // HEAD
    vblk = C_SBV // HEAD
    row_blk = pl.BlockSpec((SB_BLK, HEAD), lambda h, i: (i, h))
    full_blk = pl.BlockSpec((lp, HEAD), lambda h, i: (0, h))
    return pl.pallas_call(
        body, name="sb_attn_bwd", grid=(N_HEADS, nq),
        in_specs=[pl.BlockSpec((SB_BLK, HEAD), lambda h, i: (i, qblk + h)),
                  pl.BlockSpec((lp, HEAD), lambda h, i: (0, kblk + h)),
                  pl.BlockSpec((lp, HEAD), lambda h, i: (0, vblk + h)),
                  row_blk, row_blk],
        out_specs=[row_blk, full_blk, full_blk],
        out_shape=[_sds((lp, N_HEADS * HEAD))] * 3,
        compiler_params=_cparams(dimension_semantics=("arbitrary", "arbitrary")),
    )(proj, proj, proj, tot, do)


ROW_TILE = 128


def _row_in_specs(rows, consts, tm):
    specs = [pl.BlockSpec((tm, r[2]), functools.partial(lambda i, cb, rb: (rb + i, cb), cb=r[1],
                                                         rb=r[3] if len(r) > 3 else 0)) for r in rows]
    specs += [pl.BlockSpec(c.shape, lambda i: (0, 0)) for c in consts]
    return specs


def rowwise(fn, name, rows, consts, outs, tm=ROW_TILE, n_rows=None):
    n_in = len(rows) + len(consts)
    lp = rows[0][0].shape[0] if n_rows is None else n_rows

    def body(*refs):
        res = fn(*[r[...].astype(F32) for r in refs[:n_in]])
        for o_ref, val in zip(refs[n_in:], res, strict=True):
            o_ref[...] = val.astype(o_ref.dtype)

    return pl.pallas_call(
        body, name=name, grid=(lp // tm,),
        in_specs=_row_in_specs(rows, consts, tm),
        out_specs=[pl.BlockSpec((tm, w), lambda i: (i, 0)) for w, _ in outs],
        out_shape=[_sds((lp, w), dt) for w, dt in outs],
        compiler_params=_cparams(dimension_semantics=("arbitrary",)),
    )(*[r[0] for r in rows], *consts)


def rowwise_vjp(fn, name, rows, consts, cots, d_dtypes, tm=ROW_TILE):
    n_r, n_c, n_o = len(rows), len(consts), len(cots)
    lp = rows[0][0].shape[0]
    wanted = [k for k, dt in enumerate(d_dtypes) if dt is not None]

    def body(*refs):
        ins = [r[...].astype(F32) for r in refs[:n_r + n_c]]
        cot = tuple(r[...].astype(F32) for r in refs[n_r + n_c:n_r + n_c + n_o])
        out_refs = refs[n_r + n_c + n_o:]
        _, vjp = jax.vjp(fn, *ins)
        grads = vjp(cot)
        for o_ref, k in zip(out_refs[:len(wanted)], wanted):
            o_ref[...] = grads[k].astype(o_ref.dtype)

        @pl.when(pl.program_id(0) == 0)
        def _():
            for o_ref in out_refs[len(wanted):]:
                o_ref[...] = jnp.zeros(o_ref.shape, F32)

        for o_ref, g in zip(out_refs[len(wanted):], grads[n_r:], strict=True):
            o_ref[...] += g

    return pl.pallas_call(
        body, name=name, grid=(lp // tm,),
        in_specs=_row_in_specs(rows, consts, tm) + [pl.BlockSpec((tm, c.shape[1]), lambda i: (i, 0)) for c in cots],
        out_specs=[pl.BlockSpec((tm, rows[k][2]), lambda i: (i, 0)) for k in wanted]
        + [pl.BlockSpec(c.shape, lambda i: (0, 0)) for c in consts],
        out_shape=[_sds((lp, rows[k][2]), d_dtypes[k]) for k in wanted] + [_sds(c.shape) for c in consts],
        compiler_params=_cparams(dimension_semantics=("arbitrary",)),
    )(*[r[0] for r in rows], *consts, *cots)


def _rms(t, eps):
    return t * lax.rsqrt(jnp.mean(t * t, axis=-1, keepdims=True) + eps)


def f_rmsnorm(h, g):
    return (_rms(h, RMS_EPS) * g,)


def _per_head(t, fn):
    return jnp.concatenate([fn(t[:, h * HEAD:(h + 1) * HEAD]) for h in range(t.shape[1] // HEAD)], axis=1)


_expand_beta = make_expand(0, HEAD, N_HEADS * HEAD)
_expand_g = make_expand(8, HEAD, N_HEADS * HEAD)
_expand_dt = make_expand(16, SSM_P, SSM_INNER)
_expand_d = make_expand(0, SSM_P, SSM_INNER)


def f_prep(cq, ck, x, misc, bias_row, alog_row):
    def l2(t):
        return t * lax.rsqrt(jnp.sum(t * t, axis=-1, keepdims=True) + L2_EPS)

    qh = _per_head(cq, l2) * (HEAD ** -0.5)
    kh = _per_head(ck, l2)
    step = _softplus(misc + bias_row)
    decay = -jnp.exp(alog_row) * step
    bx = _expand_beta(_sigmoid(misc))
    gx = _expand_g(decay)
    xs = x * _expand_dt(step)
    la = _expand_dt(decay)
    return qh, kh, bx, gx, xs, la


def f_gate(o_a, sb_z, o_b, gd_z, y, x, ss_z, gdn_g, ssm_g, d_row):
    a_in = o_a * _silu(sb_z)
    b_in = _per_head(o_b, lambda t: _rms(t, RMS_EPS) * gdn_g) * _silu(gd_z)
    y2 = (y + _expand_d(d_row) * x) * _silu(ss_z)
    half = SSM_INNER // SSM_GROUPS
    c_in = jnp.concatenate([_rms(y2[:, g * half:(g + 1) * half], RMS_EPS) * ssm_g[:, g * half:(g + 1) * half]
                            for g in range(SSM_GROUPS)], axis=1)
    return a_in, b_in, c_in


def f_mix(pa, pb, pc, ga, gb, gc):
    return (_sigmoid(ga) * pa + _sigmoid(gb) * pb + _sigmoid(gc) * pc,)


def f_add(a, b):
    return (a + b,)


def final_loss(h, tgt, mask, g):
    lp = h.shape[0]
    tm = ROW_TILE

    def body(h_ref, t_ref, m_ref, g_ref, loss_ref, dh_ref, dg_ref):
        tgt, msk = t_ref[...], m_ref[:, :1]

        def f(h, g):
            err = _rms(h, RMS_EPS) * g - tgt
            return jnp.sum(0.5 * jnp.mean(err * err, axis=-1, keepdims=True) * msk, axis=0, keepdims=True)

        val, vjp = jax.vjp(f, h_ref[...], g_ref[...])
        dh, dg = vjp(jnp.ones((1, 1), F32))
        dh_ref[...] = dh

        @pl.when(pl.program_id(0) == 0)
        def _():
            loss_ref[...] = jnp.zeros(loss_ref.shape, F32)
            dg_ref[...] = jnp.zeros(dg_ref.shape, F32)

        loss_ref[...] += jnp.broadcast_to(val, loss_ref.shape)
        dg_ref[...] += dg

    d = h.shape[1]
    return pl.pallas_call(
        body, name="final_loss", grid=(lp // tm,),
        in_specs=[pl.BlockSpec((tm, d), lambda i: (i, 0)), pl.BlockSpec((tm, d), lambda i: (i, 0)),
                  pl.BlockSpec((tm, LANE), lambda i: (i, 0)), pl.BlockSpec((1, d), lambda i: (0, 0))],
        out_specs=[pl.BlockSpec((1, LANE), lambda i: (0, 0)), pl.BlockSpec((tm, d), lambda i: (i, 0)),
                   pl.BlockSpec((1, d), lambda i: (0, 0))],
        out_shape=[_sds((1, LANE)), _sds((lp, d)), _sds((1, d))],
        compiler_params=_cparams(dimension_semantics=("arbitrary",)),
    )(h, tgt, mask, g)


CONV_TILE = 512


def _shift_down(x, s):
    if s == 0:
        return x
    return jnp.where(_iota(x.shape, 0) >= s, pltpu.roll(x, s, 0), 0.0)


def _shift_up(x, s):
    if s == 0:
        return x
    n = x.shape[0]
    return jnp.where(_iota(x.shape, 0) < n - s, pltpu.roll(x, n - s, 0), 0.0)


def _conv_pre(x, w_ref, b):
    pre = b + w_ref[CONV_K - 1:CONV_K, :] * x
    for k in range(CONV_K - 1):
        pre = pre + w_ref[k:k + 1, :] * _shift_down(x, CONV_K - 1 - k)
    return pre


def conv_silu_fwd(x, col, width, w, b, name):
    lp = x.shape[0]
    ct = CONV_TILE if width % CONV_TILE == 0 else LANE * 2

    def body(x_ref, w_ref, b_ref, y_ref):
        y_ref[...] = _silu(_conv_pre(x_ref[...], w_ref, b_ref[...]))

    return pl.pallas_call(
        body, name=name, grid=(width // ct,),
        in_specs=[pl.BlockSpec((lp, ct), lambda j: (0, col // ct + j)),
                  pl.BlockSpec((CONV_K, ct), lambda j: (0, j)), pl.BlockSpec((1, ct), lambda j: (0, j))],
        out_specs=pl.BlockSpec((lp, ct), lambda j: (0, j)),
        out_shape=_sds((lp, width)),
        compiler_params=_cparams(dimension_semantics=("arbitrary",)),
    )(x, w, b)


def conv_silu_bwd(x, col, width, w, b, dy, name):
    lp = x.shape[0]
    ct = CONV_TILE if width % CONV_TILE == 0 else LANE * 2

    def body(x_ref, w_ref, b_ref, dy_ref, dx_ref, dw_ref, db_ref):
        x = x_ref[...]
        pre = _conv_pre(x, w_ref, b_ref[...])
        sg = _sigmoid(pre)
        dpre = dy_ref[...] * (sg * (1.0 + pre * (1.0 - sg)))
        dx = w_ref[CONV_K - 1:CONV_K, :] * dpre
        for k in range(CONV_K - 1):
            dx = dx + w_ref[k:k + 1, :] * _shift_up(dpre, CONV_K - 1 - k)
        dx_ref[...] = dx.astype(dx_ref.dtype)
        for k in range(CONV_K):
            dw_ref[k:k + 1, :] = jnp.sum(dpre * _shift_down(x, CONV_K - 1 - k), axis=0, keepdims=True)
        db_ref[...] = jnp.sum(dpre, axis=0, keepdims=True)

    return pl.pallas_call(
        body, name=name, grid=(width // ct,),
        in_specs=[pl.BlockSpec((lp, ct), lambda j: (0, col // ct + j)),
                  pl.BlockSpec((CONV_K, ct), lambda j: (0, j)), pl.BlockSpec((1, ct), lambda j: (0, j)),
                  pl.BlockSpec((lp, ct), lambda j: (0, j))],
        out_specs=[pl.BlockSpec((lp, ct), lambda j: (0, j)), pl.BlockSpec((CONV_K, ct), lambda j: (0, j)),
                   pl.BlockSpec((1, ct), lambda j: (0, j))],
        out_shape=[_sds((lp, width), BF16), _sds((CONV_K, width)), _sds((1, width))],
        compiler_params=_cparams(dimension_semantics=("arbitrary",)),
    )(x, w, b, dy)


def matmul(a, b, out_dtype, tn, name, add=None, tk=None):
    m, kdim = a.shape
    n = b.shape[1]
    if tk is None:
        def body(a_ref, b_ref, *rest):
            acc = _dg(_bf(a_ref[...]), _bf(b_ref[...]), 1, 0)
            if add is not None:
                acc = acc + rest[0][...]
            rest[-1][...] = acc.astype(out_dtype)

        in_specs = [pl.BlockSpec((m, kdim), lambda j: (0, 0)), pl.BlockSpec((kdim, tn), lambda j: (0, j))]
        if add is not None:
            in_specs.append(pl.BlockSpec((m, tn), lambda j: (0, j)))
        return pl.pallas_call(
            body, name=name, grid=(n // tn,), in_specs=in_specs,
            out_specs=pl.BlockSpec((m, tn), lambda j: (0, j)), out_shape=_sds((m, n), out_dtype),
            compiler_params=_cparams(dimension_semantics=("arbitrary",)),
        )(a, b, *([] if add is None else [add]))

    nk = kdim // tk

    def body_k(a_ref, b_ref, *rest):
        o_ref, acc_ref = rest[-2], rest[-1]
        k = pl.program_id(1)

        @pl.when(k == 0)
        def _():
            acc_ref[...] = jnp.zeros(acc_ref.shape, F32)

        acc_ref[...] += _dg(_bf(a_ref[...]), _bf(b_ref[...]), 1, 0)

        @pl.when(k == nk - 1)
        def _():
            acc = acc_ref[...]
            if add is not None:
                acc = acc + rest[0][...]
            o_ref[...] = acc.astype(out_dtype)

    in_specs = [pl.BlockSpec((m, tk), lambda j, k: (0, k)), pl.BlockSpec((tk, tn), lambda j, k: (k, j))]
    if add is not None:
        in_specs.append(pl.BlockSpec((m, tn), lambda j, k: (0, j)))
    return pl.pallas_call(
        body_k, name=name, grid=(n // tn, nk), in_specs=in_specs,
        out_specs=pl.BlockSpec((m, tn), lambda j, k: (0, j)), out_shape=_sds((m, n), out_dtype),
        scratch_shapes=[pltpu.VMEM((m, tn), F32)],
        compiler_params=_cparams(dimension_semantics=("arbitrary", "arbitrary")),
    )(a, b, *([] if add is None else [add]))


def f_adamw(w, g, m, v):
    m = ADAM_B1 * m + (1.0 - ADAM_B1) * g
    v = ADAM_B2 * v + (1.0 - ADAM_B2) * (g * g)
    m_hat = m / (1.0 - ADAM_B1 ** ADAM_STEP)
    v_hat = v / (1.0 - ADAM_B2 ** ADAM_STEP)
    delta = -ADAM_LR * (m_hat / (jnp.sqrt(v_hat) + ADAM_EPS) + ADAM_WD * w)
    return delta, m, v


def adamw(w, g, m, v, name):
    shape = w.shape
    cols = shape[-1]
    rows = math.prod(shape[:-1]) if len(shape) > 1 else 1
    tm = next((t for t in (256, 128, 64, 32, 16, 8) if rows % t == 0), rows)
    args = [(t.reshape(rows, cols), 0, cols) for t in (w, g, m, v)]
    outs = rowwise(f_adamw, name, args, [], [(cols, F32)] * 3, tm=tm)
    return [o.reshape(shape) for o in outs]


_MESH = pl.DeviceIdType.MESH
_HBM = pl.BlockSpec(memory_space=pltpu.HBM)


def _my_pos():
    return lax.axis_index("x"), lax.axis_index("y"), lax.axis_index("c")


def _linear(px, py, pc):
    return 4 * px + 2 * py + pc


def all_gather(x, name):
    def body(x_ref, out_ref, send_sems, recv_sems, local_sem):
        x, y, c = _my_pos()
        me, sibling = (x, y, c), (x, y, 1 - c)
        chips = [(1 - x, y), (x, 1 - y), (1 - x, 1 - y)]

        def slab(px, py, pc):
            return out_ref.at[_linear(px, py, pc)]

        def copy(k, block, to, src=None):
            return pltpu.make_async_remote_copy(
                src_ref=slab(*block) if src is None else src, dst_ref=slab(*block),
                send_sem=send_sems.at[k], recv_sem=recv_sems.at[k], device_id=to, device_id_type=_MESH)

        mine = pltpu.make_async_copy(x_ref, slab(*me), local_sem)
        mine.start()
        first = [copy(0, me, sibling, src=x_ref)]
        first += [copy(1 + j, me, (*chip, c), src=x_ref) for j, chip in enumerate(chips)]
        for cp in first:
            cp.start()
        passed = [copy(4 + j, (*chip, c), sibling) for j, chip in enumerate(chips)]
        for j, chip in enumerate(chips):
            copy(1 + j, (*chip, c), me).wait_recv()
            passed[j].start()
        copy(0, sibling, me).wait_recv()
        for j, chip in enumerate(chips):
            copy(4 + j, (*chip, 1 - c), me).wait_recv()
        for cp in first + passed:
            cp.wait_send()
        mine.wait()

    return pl.pallas_call(
        body, name=name, out_shape=_sds((N_DEV,) + x.shape, x.dtype), in_specs=[_HBM], out_specs=_HBM,
        scratch_shapes=[pltpu.SemaphoreType.DMA((7,)), pltpu.SemaphoreType.DMA((7,)), pltpu.SemaphoreType.DMA],
    )(x)


def scatter_blocks(g, name):
    def body(g_ref, out_ref, send_sems, recv_sems):
        x, y, c = _my_pos()
        copies = []
        for r in range(1, N_DEV):
            peer = (1 - x if r & 4 else x, 1 - y if r & 2 else y, 1 - c if r & 1 else c)
            copies.append(pltpu.make_async_remote_copy(
                src_ref=g_ref.at[_linear(*peer)], dst_ref=out_ref.at[r - 1],
                send_sem=send_sems.at[r - 1], recv_sem=recv_sems.at[r - 1], device_id=peer, device_id_type=_MESH))
        for cp in copies:
            cp.start()
        for cp in copies:
            cp.wait()

    return pl.pallas_call(
        body, name=name, out_shape=_sds((N_DEV - 1,) + g.shape[1:], g.dtype), in_specs=[_HBM], out_specs=_HBM,
        scratch_shapes=[pltpu.SemaphoreType.DMA((7,)), pltpu.SemaphoreType.DMA((7,))],
    )(g)


def f_sum8(*xs):
    acc = xs[0]
    for t in xs[1:]:
        acc = acc + t
    return (acc,)


W_IN_SHARD = D_IN // N_DEV
PACK_ROWS = 2688
_PACK_SPLITS = (W_IN_SHARD, 128, 128, 256, 128)


def pack_layer(w_in, wa, wb, wc, wo):
    parts = [w_in.reshape(W_IN_SHARD, D_MODEL), wa, wb, wc, wo]
    rows = sum(_PACK_SPLITS)
    return jnp.concatenate(parts + [jnp.zeros((PACK_ROWS - rows, D_MODEL), w_in.dtype)], axis=0)


def unpack_layer(p):
    out, r = [], 0
    for n in _PACK_SPLITS:
        out.append(p[r:r + n])
        r += n
    out[0] = out[0].reshape(D_MODEL, W_IN_SHARD)
    return out


def to_kernel_columns(w):
    k = w.shape[0]
    out = jnp.zeros((k, D_INC), w.dtype)
    for src, width, dst in _SEGS:
        out = lax.dynamic_update_slice(out, w[:, src:src + width], (0, dst))
    return out


def from_kernel_columns(w):
    parts = sorted(_SEGS)
    return jnp.concatenate([w[:, dst:dst + width] for _, width, dst in parts], axis=1)


def _lane_row(pieces):
    row = jnp.zeros((1, LANE), F32)
    for first, vec in pieces:
        row = lax.dynamic_update_slice(row, vec.reshape(1, -1).astype(F32), (0, first))
    return row


_D = D_MODEL


def layer_fwd(h, p):
    (u,) = rowwise(f_rmsnorm, "rmsnorm_fwd", [(h, 0, _D)], [p["norm_g"]], [(_D, BF16)])
    proj = matmul(u, p["w_in"], F32, 640, "in_proj")
    cqkv = conv_silu_fwd(proj, C_GDQ, GDN_CONV_CH, p["gdn_conv_w"], jnp.zeros((1, GDN_CONV_CH), F32), "gdn_conv_fwd")
    xbc = conv_silu_fwd(proj, C_XBC, SSM_CONV_CH, p["ssm_conv_w"], p["ssm_conv_b"], "ssm_conv_fwd")
    prep_rows = [(cqkv, 0, 1024), (cqkv, 1, 1024), (xbc, 0, SSM_INNER), (proj, C_MISC // LANE, LANE)]
    prep_consts = [p["bias_row"], p["alog_row"]]
    qh, kh, bx, gx, xs, la = rowwise(f_prep, "prep_fwd", prep_rows, prep_consts,
                                     [(1024, F32)] * 4 + [(SSM_INNER, F32)] * 2)
    o_a, tot = sb_attn_fwd(proj)
    o_b, st_g = gdn_scan_fwd(qh, kh, cqkv, bx, gx)
    y, st_s = ssd_scan_fwd(xs, la, xbc)
    gate_rows = [(o_a, 0, 1024), (proj, C_SBZ // 1024, 1024), (o_b, 0, 1024), (proj, C_GDZ // 1024, 1024),
                 (y, 0, SSM_INNER), (xbc, 0, SSM_INNER), (proj, C_SSZ // SSM_INNER, SSM_INNER)]
    gate_consts = [p["gdn_norm_g"], p["ssm_norm_g"], p["d_row"]]
    a_in, b_in, c_in = rowwise(f_gate, "gate_fwd", gate_rows, gate_consts,
                               [(1024, BF16), (1024, BF16), (SSM_INNER, BF16)])
    pa = matmul(a_in, p["wa"], F32, 512, "branch_a")
    pb = matmul(b_in, p["wb"], F32, 512, "branch_b")
    pc = matmul(c_in, p["wc"], F32, 512, "branch_c")
    mix_rows = [(pa, 0, _D), (pb, 0, _D), (pc, 0, _D)] + [(proj, C_GATE // _D + k, _D) for k in range(3)]
    (merged,) = rowwise(f_mix, "mix_fwd", mix_rows, [], [(_D, BF16)])
    h_new = matmul(merged, p["wo"], F32, 512, "out_proj", add=h)
    saved = dict(h=h, u=u, proj=proj, cqkv=cqkv, xbc=xbc, qh=qh, kh=kh, bx=bx, gx=gx, xs=xs, la=la, o_a=o_a, tot=tot,
                 o_b=o_b, st_g=st_g, y=y, st_s=st_s, a_in=a_in, b_in=b_in, c_in=c_in, pa=pa, pb=pb, pc=pc,
                 merged=merged, prep_rows=prep_rows, prep_consts=prep_consts, gate_rows=gate_rows,
                 gate_consts=gate_consts, mix_rows=mix_rows)
    return h_new, saved


def layer_bwd(dh, s, p):
    proj = s["proj"]
    dmerged = matmul(dh, p["wo_t"], F32, 512, "d_merged")
    g = {"wo": matmul(s["merged"].T, dh, F32, 512, "dw_out")}
    dpa, dpb, dpc, dga, dgb, dgc = rowwise_vjp(f_mix, "mix_bwd", s["mix_rows"], [], [dmerged], [BF16] * 6)
    da_in = matmul(dpa, p["wa_t"], F32, 512, "d_branch_a")
    db_in = matmul(dpb, p["wb_t"], F32, 512, "d_branch_b")
    dc_in = matmul(dpc, p["wc_t"], F32, 512, "d_branch_c")
    g["wa"] = matmul(s["a_in"].T, dpa, F32, 512, "dw_branch_a")
    g["wb"] = matmul(s["b_in"].T, dpb, F32, 512, "dw_branch_b")
    g["wc"] = matmul(s["c_in"].T, dpc, F32, 512, "dw_branch_c")
    (d_oa, d_sbz, d_ob, d_gdz, dy, dx_gate, d_ssz, g["gdn_norm_g"], g["ssm_norm_g"], g["d_row"]) = rowwise_vjp(
        f_gate, "gate_bwd", s["gate_rows"], s["gate_consts"], [da_in, db_in, dc_in],
        [F32, BF16, F32, BF16, F32, F32, BF16])
    dq_sb, dk_sb, dv_sb = sb_attn_bwd(proj, s["tot"], d_oa)
    dqh, dkh, dv_g, dbx, dgx = gdn_scan_bwd(s["qh"], s["kh"], s["cqkv"], s["bx"], s["gx"], s["st_g"], d_ob)
    dxs, dla, db_s, dc_s = ssd_scan_bwd(s["xs"], s["la"], s["xbc"], s["st_s"], dy)
    dcq, dck, dx_prep, dmisc, g["bias_row"], g["alog_row"] = rowwise_vjp(
        f_prep, "prep_bwd", s["prep_rows"], s["prep_consts"], [dqh, dkh, dbx, dgx, dxs, dla], [F32, F32, F32, BF16])
    d_cqkv = jnp.concatenate([dcq, dck, dv_g], axis=1)
    din_g, g["gdn_conv_w"], _ = conv_silu_bwd(proj, C_GDQ, GDN_CONV_CH, p["gdn_conv_w"],
                                              jnp.zeros((1, GDN_CONV_CH), F32), d_cqkv, "gdn_conv_bwd")
    (dx,) = rowwise(f_add, "add_dx", [(dx_gate, 0, SSM_INNER), (dx_prep, 0, SSM_INNER)], [], [(SSM_INNER, F32)])
    d_xbc = jnp.concatenate([dx, db_s, dc_s], axis=1)
    din_s, g["ssm_conv_w"], g["ssm_conv_b"] = conv_silu_bwd(proj, C_XBC, SSM_CONV_CH, p["ssm_conv_w"],
                                                            p["ssm_conv_b"], d_xbc, "ssm_conv_bwd")
    pad = jnp.zeros((proj.shape[0], D_INC - C_MISC - LANE), BF16)
    dproj = jnp.concatenate([_bf(dq_sb), _bf(dk_sb), _bf(dv_sb), d_sbz, din_g, d_gdz, d_ssz, dga, dgb, dgc, din_s,
                             dmisc] + ([pad] if pad.shape[1] else []), axis=1)
    du = matmul(dproj, p["w_in_t"], F32, _D, "d_u", tk=640)
    g["w_in"] = matmul(s["u"].T, dproj, F32, 640, "dw_in")
    dh_norm, g["norm_g"] = rowwise_vjp(f_rmsnorm, "rmsnorm_bwd", [(s["h"], 0, _D)], [p["norm_g"]], [du], [F32])
    (dh_in,) = rowwise(f_add, "add_dh", [(dh, 0, _D), (dh_norm, 0, _D)], [], [(_D, F32)])
    return dh_in, g


_REPLICATED = ("norm_g", "gdn_a_log", "gdn_dt_bias", "gdn_norm_g", "ssm_conv_b", "ssm_a_log", "ssm_dt_bias", "ssm_d",
               "ssm_norm_g", "final_norm_g")
_WEIGHTS = ("meta_tokens", "norm_g", "w_in", "gdn_conv_w", "gdn_a_log", "gdn_dt_bias", "gdn_norm_g", "ssm_conv_w",
            "ssm_conv_b", "ssm_a_log", "ssm_dt_bias", "ssm_d", "ssm_norm_g", "w_branch_a", "w_branch_b", "w_branch_c",
            "w_out", "final_norm_g")
SMALL_ROWS = 128


def _flatten_small(parts):
    flat = jnp.concatenate([t.reshape(-1).astype(F32) for t in parts])
    return jnp.pad(flat, (0, SMALL_ROWS * _D - flat.shape[0])).reshape(SMALL_ROWS, _D)


def _unflatten_small(block, shapes):
    flat, out, at = block.reshape(-1), [], 0
    for shp in shapes:
        n = math.prod(shp)
        out.append(flat[at:at + n].reshape(shp))
        at += n
    return out


def kernel(x, meta_tokens, norm_g, w_in, gdn_conv_w, gdn_a_log, gdn_dt_bias, gdn_norm_g, ssm_conv_w, ssm_conv_b, ssm_a_log, ssm_dt_bias, ssm_d, ssm_norm_g, w_branch_a, w_branch_b, w_branch_c, w_out, final_norm_g, loss_target, m_meta_tokens, m_norm_g, m_w_in, m_gdn_conv_w, m_gdn_a_log, m_gdn_dt_bias, m_gdn_norm_g, m_ssm_conv_w, m_ssm_conv_b, m_ssm_a_log, m_ssm_dt_bias, m_ssm_d, m_ssm_norm_g, m_w_branch_a, m_w_branch_b, m_w_branch_c, m_w_out, m_final_norm_g, v_meta_tokens, v_norm_g, v_w_in, v_gdn_conv_w, v_gdn_a_log, v_gdn_dt_bias, v_gdn_norm_g, v_ssm_conv_w, v_ssm_conv_b, v_ssm_a_log, v_ssm_dt_bias, v_ssm_d, v_ssm_norm_g, v_w_branch_a, v_w_branch_b, v_w_branch_c, v_w_out, v_final_norm_g):
    weights = dict(meta_tokens=meta_tokens, norm_g=norm_g, w_in=w_in, gdn_conv_w=gdn_conv_w, gdn_a_log=gdn_a_log,
                   gdn_dt_bias=gdn_dt_bias, gdn_norm_g=gdn_norm_g, ssm_conv_w=ssm_conv_w, ssm_conv_b=ssm_conv_b,
                   ssm_a_log=ssm_a_log, ssm_dt_bias=ssm_dt_bias, ssm_d=ssm_d, ssm_norm_g=ssm_norm_g,
                   w_branch_a=w_branch_a, w_branch_b=w_branch_b, w_branch_c=w_branch_c, w_out=w_out,
                   final_norm_g=final_norm_g)
    m_in = dict(zip(_WEIGHTS, (m_meta_tokens, m_norm_g, m_w_in, m_gdn_conv_w, m_gdn_a_log, m_gdn_dt_bias, m_gdn_norm_g,
                               m_ssm_conv_w, m_ssm_conv_b, m_ssm_a_log, m_ssm_dt_bias, m_ssm_d, m_ssm_norm_g,
                               m_w_branch_a, m_w_branch_b, m_w_branch_c, m_w_out, m_final_norm_g)))
    v_in = dict(zip(_WEIGHTS, (v_meta_tokens, v_norm_g, v_w_in, v_gdn_conv_w, v_gdn_a_log, v_gdn_dt_bias, v_gdn_norm_g,
                               v_ssm_conv_w, v_ssm_conv_b, v_ssm_a_log, v_ssm_dt_bias, v_ssm_d, v_ssm_norm_g,
                               v_w_branch_a, v_w_branch_b, v_w_branch_c, v_w_out, v_final_norm_g)))
    depth = w_in.shape[0]
    seq = x.shape[1]
    tokens = N_META + seq
    lp = -(-tokens // SB_BLK) * SB_BLK
    me = _linear(*_my_pos())

    packed = jnp.concatenate([pack_layer(_bf(w_in[l]), _bf(w_branch_a[l]), _bf(w_branch_b[l]), _bf(w_branch_c[l]),
                                         _bf(w_out[l])) for l in range(depth)], axis=0)
    big = all_gather(packed, "gather_weights").reshape(N_DEV, depth, PACK_ROWS, _D)
    small_in = jnp.concatenate([gdn_conv_w.reshape(-1), ssm_conv_w.reshape(-1), meta_tokens.reshape(-1)])
    small_in = small_in.reshape(-1, LANE)
    small = all_gather(small_in, "gather_small").reshape(N_DEV, -1)
    n_g, n_s = gdn_conv_w.size, ssm_conv_w.size
    gdn_w_full = small[:, :n_g].reshape((N_DEV,) + gdn_conv_w.shape).transpose(1, 2, 0, 3).reshape(depth, CONV_K, -1)
    ssm_w_full = small[:, n_g:n_g + n_s].reshape((N_DEV,) + ssm_conv_w.shape).transpose(1, 2, 0, 3)
    ssm_w_full = ssm_w_full.reshape(depth, CONV_K, -1)
    meta_full = small[:, n_g + n_s:].reshape(N_DEV, N_META, -1).transpose(1, 0, 2).reshape(N_META, _D)

    params = []
    for l in range(depth):
        blocks = [unpack_layer(big[d, l]) for d in range(N_DEV)]
        w_full = jnp.concatenate([b[0] for b in blocks], axis=1)
        wk = to_kernel_columns(w_full)
        wa, wb, wc, wo = (jnp.concatenate([b[k] for b in blocks], axis=0) for k in range(1, 5))
        params.append(dict(
            norm_g=norm_g[l].reshape(1, _D), w_in=wk, w_in_t=wk.T, gdn_conv_w=gdn_w_full[l], ssm_conv_w=ssm_w_full[l],
            ssm_conv_b=ssm_conv_b[l].reshape(1, -1),
            bias_row=_lane_row([(8, gdn_dt_bias[l]), (16, ssm_dt_bias[l])]),
            alog_row=_lane_row([(8, gdn_a_log[l]), (16, ssm_a_log[l])]),
            gdn_norm_g=gdn_norm_g[l].reshape(1, HEAD), ssm_norm_g=ssm_norm_g[l].reshape(1, SSM_INNER),
            d_row=_lane_row([(0, ssm_d[l])]), wa=wa, wb=wb, wc=wc, wo=wo, wa_t=wa.T, wb_t=wb.T, wc_t=wc.T, wo_t=wo.T))

    h = jnp.concatenate([meta_full, x[0], jnp.zeros((lp - tokens, _D), F32)], axis=0)
    tgt = jnp.pad(loss_target[0], ((N_META, lp - tokens), (0, 0)))
    rows = lax.broadcasted_iota(jnp.int32, (lp, LANE), 0)
    mask = jnp.logical_and(rows >= N_META, rows < tokens).astype(F32)

    saved = []
    for l in range(depth):
        h, s = layer_fwd(h, params[l])
        saved.append(s)
    loss_row, dh, d_final_g = final_loss(h, tgt, mask, final_norm_g.reshape(1, _D))
    grads = [None] * depth
    for l in reversed(range(depth)):
        dh, grads[l] = layer_bwd(dh, saved[l], params[l])

    send = []
    gw_full = [from_kernel_columns(grads[l]["w_in"]) for l in range(depth)]
    for d in range(N_DEV):
        per_layer = []
        for l in range(depth):
            g = grads[l]
            gw = gw_full[l][:, d * W_IN_SHARD:(d + 1) * W_IN_SHARD]
            per_layer.append(pack_layer(gw, g["wa"][d * 128:(d + 1) * 128], g["wb"][d * 128:(d + 1) * 128],
                                        g["wc"][d * 256:(d + 1) * 256], g["wo"][d * 128:(d + 1) * 128]))
        send.append(jnp.concatenate(per_layer, axis=0))
    send = jnp.stack(send)
    own = lax.dynamic_index_in_dim(send, me, axis=0, keepdims=False)
    got = scatter_blocks(_bf(send), "scatter_grads")
    n_rows = depth * PACK_ROWS
    got2 = got.reshape((N_DEV - 1) * n_rows, _D)
    sum_rows = [(own, 0, _D)] + [(got2, 0, _D, r * (n_rows // ROW_TILE)) for r in range(N_DEV - 1)]
    (gsum,) = rowwise(f_sum8, "sum_grads", sum_rows, [], [(_D, F32)], n_rows=n_rows)
    gsum = gsum.reshape(depth, PACK_ROWS, _D)
    big_g = [unpack_layer(gsum[l]) for l in range(depth)]
    grad = dict(w_in=jnp.stack([b[0] for b in big_g]), w_branch_a=jnp.stack([b[1] for b in big_g]),
                w_branch_b=jnp.stack([b[2] for b in big_g]), w_branch_c=jnp.stack([b[3] for b in big_g]),
                w_out=jnp.stack([b[4] for b in big_g]))

    def stack(name, pick=lambda t: t):
        return jnp.stack([pick(grads[l][name]) for l in range(depth)])

    small_parts = [
        stack("norm_g"), stack("alog_row", lambda t: t[0, 8:16]), stack("bias_row", lambda t: t[0, 8:16]),
        stack("gdn_norm_g"), stack("ssm_conv_b"), stack("alog_row", lambda t: t[0, 16:48]),
        stack("bias_row", lambda t: t[0, 16:48]), stack("d_row", lambda t: t[0, :SSM_HEADS]), stack("ssm_norm_g"),
        d_final_g, stack("gdn_conv_w"), stack("ssm_conv_w"), dh[:N_META], loss_row[0, :1]]
    small_shapes = [(depth, _D), (depth, 8), (depth, 8), (depth, HEAD), (depth, SSM_CONV_CH), (depth, SSM_HEADS),
                    (depth, SSM_HEADS), (depth, SSM_HEADS), (depth, SSM_INNER), (_D,), (depth, CONV_K, GDN_CONV_CH),
                    (depth, CONV_K, SSM_CONV_CH), (N_META, _D), ()]
    small_all = all_gather(_flatten_small(small_parts), "gather_small_grads")
    (small_sum,) = rowwise(f_sum8, "sum_small", [(small_all.reshape(N_DEV * SMALL_ROWS, _D), 0, _D, d)
                                                 for d in range(N_DEV)], [], [(_D, F32)], n_rows=SMALL_ROWS)
    (grad["norm_g"], grad["gdn_a_log"], grad["gdn_dt_bias"], grad["gdn_norm_g"], grad["ssm_conv_b"],
     grad["ssm_a_log"], grad["ssm_dt_bias"], grad["ssm_d"], grad["ssm_norm_g"], grad["final_norm_g"], gdn_cw, ssm_cw,
     meta_g, loss) = _unflatten_small(small_sum, small_shapes)
    grad["gdn_conv_w"] = lax.dynamic_slice_in_dim(gdn_cw, me * gdn_conv_w.shape[2], gdn_conv_w.shape[2], axis=2)
    grad["ssm_conv_w"] = lax.dynamic_slice_in_dim(ssm_cw, me * ssm_conv_w.shape[2], ssm_conv_w.shape[2], axis=2)
    grad["meta_tokens"] = lax.dynamic_slice_in_dim(meta_g, me * meta_tokens.shape[1], meta_tokens.shape[1], axis=1)

    delta, new_m, new_v = {}, {}, {}
    for name in _WEIGHTS:
        delta[name], new_m[name], new_v[name] = adamw(weights[name], grad[name], m_in[name], v_in[name],
                                                      "adamw_" + name)
    grad_x = dh[N_META:tokens][None]
    return (loss, grad_x, *[grad[n] for n in _WEIGHTS], *[delta[n] for n in _WEIGHTS],
            *[new_m[n] for n in _WEIGHTS], *[new_v[n] for n in _WEIGHTS])
```

```python
import functools
import math

import jax
import jax.numpy as jnp
from jax import lax
from jax.experimental import pallas as pl
from jax.experimental.pallas import tpu as pltpu

F32 = jnp.float32
BF16 = jnp.bfloat16

N_DEV = 8
N_META = 16
D_MODEL = 1024
DEPTH = 4
RMS_EPS = 1e-6
L2_EPS = 1e-6
CONV_K = 4
HEAD = 128
N_HEADS = 8
CHUNK = 64
SSM_INNER = 2048
SSM_P = 64
SSM_HEADS = 32
SSM_N = 128
SSM_GROUPS = 2
GDN_CONV_CH = 3072
SSM_CONV_CH = 2560
LANE = 128
VMEM_LIMIT = 56 * 1024 * 1024

ADAM_LR = 0.001
ADAM_B1 = 0.9
ADAM_B2 = 0.999
ADAM_EPS = 1e-08
ADAM_WD = 0.01
ADAM_STEP = 10

D_IN = 15920
D_INC = 16000
C_SBQ, C_SBK, C_SBV, C_SBZ = 0, 1024, 2048, 3072
C_GDQ, C_GDK, C_GDV, C_GDZ = 4096, 5120, 6144, 7168
C_SSZ = 8192
C_GATE = 10240
C_XBC = 13312
C_MISC = 15872
_SEGS = (
    (0, 8192, 0),
    (8192, 16, C_MISC),
    (8208, 2048, C_SSZ),
    (10256, 2560, C_XBC),
    (12816, 32, C_MISC + 16),
    (12848, 3072, C_GATE),
)


def _cparams(**kw):
    return pltpu.CompilerParams(vmem_limit_bytes=VMEM_LIMIT, **kw)


def _bf(x):
    return x.astype(BF16)


def _dg(a, b, ca, cb):
    if a.ndim == 3:
        return lax.dot_general(a, b, (((ca + 1,), (cb + 1,)), ((0,), (0,))), preferred_element_type=F32)
    return lax.dot_general(a, b, (((ca,), (cb,)), ((), ())), preferred_element_type=F32)


@jax.custom_vjp
def mm_nn(a, b):
    return _dg(_bf(a), _bf(b), 1, 0)


def _mm_nn_fwd(a, b):
    a, b = _bf(a), _bf(b)
    return _dg(a, b, 1, 0), (a, b)


def _mm_nn_bwd(res, g):
    a, b = res
    g = _bf(g)
    return _dg(g, b, 1, 1), _dg(a, g, 0, 0)


mm_nn.defvjp(_mm_nn_fwd, _mm_nn_bwd)


@jax.custom_vjp
def mm_nt(a, b):
    return _dg(_bf(a), _bf(b), 1, 1)


def _mm_nt_fwd(a, b):
    a, b = _bf(a), _bf(b)
    return _dg(a, b, 1, 1), (a, b)


def _mm_nt_bwd(res, g):
    a, b = res
    g = _bf(g)
    return _dg(g, b, 1, 0), _dg(g, a, 0, 0)


mm_nt.defvjp(_mm_nt_fwd, _mm_nt_bwd)


@jax.custom_vjp
def mm_tn(a, b):
    return _dg(_bf(a), _bf(b), 0, 0)


def _mm_tn_fwd(a, b):
    a, b = _bf(a), _bf(b)
    return _dg(a, b, 0, 0), (a, b)


def _mm_tn_bwd(res, g):
    a, b = res
    g = _bf(g)
    return _dg(b, g, 1, 1), _dg(a, g, 1, 0)


mm_tn.defvjp(_mm_tn_fwd, _mm_tn_bwd)


def _split2(x):
    hi = _bf(x)
    return hi, _bf(x - hi.astype(F32))


def _split3(x):
    hi = _bf(x)
    r = x - hi.astype(F32)
    mid = _bf(r)
    return hi, mid, _bf(r - mid.astype(F32))


def _dg3(a, b, ca, cb):
    ah, al = _split2(a)
    bh, bl = _split2(b)
    return _dg(ah, bh, ca, cb) + (_dg(ah, bl, ca, cb) + _dg(al, bh, ca, cb))


def _iota(shape, dim):
    return lax.broadcasted_iota(jnp.int32, shape, dim)


def _const_left(c, x, ca):
    hi, mid, lo = _split3(x)
    if x.ndim == 3:
        c = jnp.broadcast_to(c, x.shape[:1] + c.shape)
    return _dg(c, hi, ca, 0) + (_dg(c, mid, ca, 0) + _dg(c, lo, ca, 0))


def _tril_incl(n):
    return (_iota((n, n), 0) >= _iota((n, n), 1)).astype(BF16)


@jax.custom_vjp
def cumsum_rows(x):
    return _const_left(_tril_incl(x.shape[-2]), x, 1)


def _cumsum_rows_fwd(x):
    return cumsum_rows(x), None


def _cumsum_rows_bwd(_, g):
    return (_const_left(_tril_incl(g.shape[-2]), g, 0),)


cumsum_rows.defvjp(_cumsum_rows_fwd, _cumsum_rows_bwd)


def _expand_mat(n_in, first, group, n_out):
    return (_iota((n_in, n_out), 0) == first + _iota((n_in, n_out), 1) // group).astype(BF16)


def _right_const(x, c, cc):
    hi, mid, lo = _split3(x)
    return _dg(hi, c, 1, cc) + (_dg(mid, c, 1, cc) + _dg(lo, c, 1, cc))


def make_expand(first, group, n_out):
    @jax.custom_vjp
    def expand(x):
        return _right_const(x, _expand_mat(x.shape[1], first, group, n_out), 0)

    def fwd(x):
        return expand(x), None

    def bwd(_, g):
        return (_right_const(g, _expand_mat(LANE, first, group, n_out), 1),)

    expand.defvjp(fwd, bwd)
    return expand


def _tri_inv_impl(m):
    n = m.shape[-1]
    r, c = _iota((n, n), 0), _iota((n, n), 1)
    eye = (r == c).astype(F32)
    d = jnp.where(r // 8 == c // 8, m, 0.0)
    t = eye - d
    p = _dg3(d, d, 1, 0)
    t = t + _dg3(t, p, 1, 0)
    p = _dg3(p, p, 1, 0)
    t = t + _dg3(t, p, 1, 0)
    for blk in (16, 32, 64):
        off = jnp.where((r // blk == c // blk) & (r // (blk // 2) != c // (blk // 2)), m, 0.0)
        t = t - _dg3(_dg3(t, off, 1, 0), t, 1, 0)
    return t


@jax.custom_vjp
def tri_inv(m):
    return _tri_inv_impl(m)


def _tri_inv_fwd(m):
    t = _tri_inv_impl(m)
    return t, t


def _tri_inv_bwd(t, g):
    return (-_dg3(_dg3(t, g, 0, 0), t, 1, 1),)


tri_inv.defvjp(_tri_inv_fwd, _tri_inv_bwd)


def _sigmoid(x):
    return 1.0 / (1.0 + jnp.exp(-x))


def _silu(x):
    return x * _sigmoid(x)


def _softplus(x):
    return jnp.maximum(x, 0.0) + jnp.log(1.0 + jnp.exp(-jnp.abs(x)))


def _row_of_diag(cb):
    n = cb.shape[-1]
    return jnp.sum(jnp.where(_iota((n, n), 0) == _iota((n, n), 1), cb, 0.0), axis=-2, keepdims=True)


def gdn_chunk(q, k, v, bx, gx, s):
    n = q.shape[-2]
    gc = cumsum_rows(gx)
    gl = jnp.sum(gx, axis=-2, keepdims=True)
    cb = gc[..., :n]
    seg = cb - _row_of_diag(cb)
    r, c = _iota((n, n), 0), _iota((n, n), 1)
    dec_strict = jnp.exp(jnp.where(r > c, seg, -1e30))
    dec_incl = jnp.exp(jnp.where(r >= c, seg, -1e30))
    kb = k * bx
    t = tri_inv(mm_nt(kb, k) * dec_strict)
    egc = jnp.exp(gc)
    u = mm_nn(t, v * bx)
    w = mm_nn(t, kb * egc)
    aqk = mm_nt(q, k) * dec_incl
    v_new = u - mm_nn(w, s)
    o = mm_nn(q * egc, s) + mm_nn(aqk, v_new)
    s_new = s * jnp.exp(gl) + mm_tn(k * jnp.exp(gl - gc), v_new)
    return o, s_new


def ssd_chunk(xs, la, b, c, st):
    n = xs.shape[-2]
    cs = cumsum_rows(la)
    cl = jnp.sum(la, axis=-2, keepdims=True)
    seg = cs - _row_of_diag(cs)
    r, cc = _iota((n, n), 0), _iota((n, n), 1)
    dec = jnp.exp(jnp.where(r >= cc, seg, -1e30))
    a = mm_nt(c, b) * dec
    y = mm_nn(a, xs) + mm_nn(c, st) * jnp.exp(cs)
    st_new = st * jnp.exp(cl) + mm_tn(b, xs * jnp.exp(cl - cs))
    return y, st_new


def _sds(shape, dtype=F32):
    return jax.ShapeDtypeStruct(shape, dtype)


def _heads(x, n):
    w = x.shape[1] // n
    return jnp.stack([x[:, h * w:(h + 1) * w] for h in range(n)])


def _unheads(x):
    return jnp.concatenate([x[h] for h in range(x.shape[0])], axis=1)


def _bcast(x, n):
    return jnp.broadcast_to(x, (n,) + x.shape)


_GDN_V_BLK = 2 * N_HEADS


def gdn_scan_fwd(q, k, v, bx, gx):
    lp = q.shape[0]
    nc = lp // CHUNK

    def body(q_ref, k_ref, v_ref, b_ref, g_ref, o_ref, st_ref, s_scr):
        @pl.when(pl.program_id(0) == 0)
        def _():
            s_scr[...] = jnp.zeros(s_scr.shape, F32)

        s = s_scr[...]
        st_ref[0] = s
        o, s_new = gdn_chunk(*[_heads(r[...], N_HEADS) for r in (q_ref, k_ref, v_ref, b_ref, g_ref)], s)
        o_ref[...] = _unheads(o)
        s_scr[...] = s_new

    width = N_HEADS * HEAD
    blk = pl.BlockSpec((CHUNK, width), lambda c: (c, 0))
    return pl.pallas_call(
        body, name="gdn_scan_fwd", grid=(nc,),
        in_specs=[blk, blk, pl.BlockSpec((CHUNK, width), lambda c: (c, _GDN_V_BLK // N_HEADS)), blk, blk],
        out_specs=[blk, pl.BlockSpec((1, N_HEADS, HEAD, HEAD), lambda c: (c, 0, 0, 0))],
        out_shape=[_sds((lp, width)), _sds((nc, N_HEADS, HEAD, HEAD))],
        scratch_shapes=[pltpu.VMEM((N_HEADS, HEAD, HEAD), F32)],
        compiler_params=_cparams(dimension_semantics=("arbitrary",)),
    )(q, k, v, bx, gx)


def gdn_scan_bwd(q, k, v, bx, gx, st, do):
    lp = q.shape[0]
    nc = lp // CHUNK

    def body(q_ref, k_ref, v_ref, b_ref, g_ref, st_ref, do_ref, dq_ref, dk_ref, dv_ref, db_ref, dg_ref, ds_scr):
        @pl.when(pl.program_id(0) == 0)
        def _():
            ds_scr[...] = jnp.zeros(ds_scr.shape, F32)

        _, vjp = jax.vjp(gdn_chunk, *[_heads(r[...], N_HEADS) for r in (q_ref, k_ref, v_ref, b_ref, g_ref)],
                         st_ref[0])
        grads = vjp((_heads(do_ref[...], N_HEADS), ds_scr[...]))
        for ref, val in zip((dq_ref, dk_ref, dv_ref, db_ref, dg_ref), grads[:5]):
            ref[...] = _unheads(val)
        ds_scr[...] = grads[5]

    width = N_HEADS * HEAD
    blk = pl.BlockSpec((CHUNK, width), lambda c: (nc - 1 - c, 0))
    return pl.pallas_call(
        body, name="gdn_scan_bwd", grid=(nc,),
        in_specs=[blk, blk, pl.BlockSpec((CHUNK, width), lambda c: (nc - 1 - c, _GDN_V_BLK // N_HEADS)), blk, blk,
                  pl.BlockSpec((1, N_HEADS, HEAD, HEAD), lambda c: (nc - 1 - c, 0, 0, 0)), blk],
        out_specs=[blk] * 5,
        out_shape=[_sds((lp, width))] * 5,
        scratch_shapes=[pltpu.VMEM((N_HEADS, HEAD, HEAD), F32)],
        compiler_params=_cparams(dimension_semantics=("arbitrary",)),
    )(q, k, v, bx, gx, st, do)


SSD_HPS = 8
_SSD_STEPS = SSM_HEADS // SSD_HPS
_XBC_B_BLK = SSM_INNER // LANE
_XBC_C_BLK = _XBC_B_BLK + SSM_GROUPS


def ssd_scan_fwd(xs, la, xbc):
    lp = xs.shape[0]
    nc = lp // CHUNK

    def body(xs_ref, la_ref, b_ref, c_ref, y_ref, st_ref, s_scr):
        j = pl.program_id(1)
        first = pl.multiple_of(j * SSD_HPS, SSD_HPS)

        @pl.when(pl.program_id(0) == 0)
        def _():
            s_scr[pl.ds(first, SSD_HPS)] = jnp.zeros((SSD_HPS, SSM_N, SSM_P), F32)

        s = s_scr[pl.ds(first, SSD_HPS)]
        st_ref[0] = s
        y, s_new = ssd_chunk(_heads(xs_ref[...], SSD_HPS), _heads(la_ref[...], SSD_HPS),
                             _bcast(b_ref[...], SSD_HPS), _bcast(c_ref[...], SSD_HPS), s)
        y_ref[...] = _unheads(y)
        s_scr[pl.ds(first, SSD_HPS)] = s_new

    width = SSD_HPS * SSM_P
    blk = pl.BlockSpec((CHUNK, width), lambda c, j: (c, j))
    per_group = _SSD_STEPS // SSM_GROUPS
    return pl.pallas_call(
        body, name="ssd_scan_fwd", grid=(nc, _SSD_STEPS),
        in_specs=[blk, blk,
                  pl.BlockSpec((CHUNK, LANE), lambda c, j: (c, _XBC_B_BLK + j // per_group)),
                  pl.BlockSpec((CHUNK, LANE), lambda c, j: (c, _XBC_C_BLK + j // per_group))],
        out_specs=[blk, pl.BlockSpec((1, SSD_HPS, SSM_N, SSM_P), lambda c, j: (c, j, 0, 0))],
        out_shape=[_sds((lp, SSM_INNER)), _sds((nc, SSM_HEADS, SSM_N, SSM_P))],
        scratch_shapes=[pltpu.VMEM((SSM_HEADS, SSM_N, SSM_P), F32)],
        compiler_params=_cparams(dimension_semantics=("arbitrary", "arbitrary")),
    )(xs, la, xbc, xbc)


def ssd_scan_bwd(xs, la, xbc, st, dy):
    lp = xs.shape[0]
    nc = lp // CHUNK
    per_group = _SSD_STEPS // SSM_GROUPS

    def body(xs_ref, la_ref, b_ref, c_ref, st_ref, dy_ref, dxs_ref, dla_ref, db_ref, dc_ref, ds_scr):
        j = pl.program_id(1)
        first = pl.multiple_of(j * SSD_HPS, SSD_HPS)

        @pl.when(pl.program_id(0) == 0)
        def _():
            ds_scr[pl.ds(first, SSD_HPS)] = jnp.zeros((SSD_HPS, SSM_N, SSM_P), F32)

        _, vjp = jax.vjp(ssd_chunk, _heads(xs_ref[...], SSD_HPS), _heads(la_ref[...], SSD_HPS),
                         _bcast(b_ref[...], SSD_HPS), _bcast(c_ref[...], SSD_HPS), st_ref[0])
        dxs, dla, db, dc, ds = vjp((_heads(dy_ref[...], SSD_HPS), ds_scr[pl.ds(first, SSD_HPS)]))
        db = jnp.sum(db, axis=0)
        dc = jnp.sum(dc, axis=0)
        dxs_ref[...] = _unheads(dxs)
        dla_ref[...] = _unheads(dla)
        ds_scr[pl.ds(first, SSD_HPS)] = ds

        @pl.when(j % per_group == 0)
        def _():
            db_ref[...] = db
            dc_ref[...] = dc

        @pl.when(j % per_group != 0)
        def _():
            db_ref[...] += db
            dc_ref[...] += dc

    width = SSD_HPS * SSM_P
    blk = pl.BlockSpec((CHUNK, width), lambda c, j: (nc - 1 - c, j))
    return pl.pallas_call(
        body, name="ssd_scan_bwd", grid=(nc, _SSD_STEPS),
        in_specs=[blk, blk,
                  pl.BlockSpec((CHUNK, LANE), lambda c, j: (nc - 1 - c, _XBC_B_BLK + j // per_group)),
                  pl.BlockSpec((CHUNK, LANE), lambda c, j: (nc - 1 - c, _XBC_C_BLK + j // per_group)),
                  pl.BlockSpec((1, SSD_HPS, SSM_N, SSM_P), lambda c, j: (nc - 1 - c, j, 0, 0)), blk],
        out_specs=[blk, blk,
                   pl.BlockSpec((CHUNK, LANE), lambda c, j: (nc - 1 - c, j // per_group)),
                   pl.BlockSpec((CHUNK, LANE), lambda c, j: (nc - 1 - c, j // per_group))],
        out_shape=[_sds((lp, SSM_INNER)), _sds((lp, SSM_INNER)),
                   _sds((lp, SSM_GROUPS * SSM_N)), _sds((lp, SSM_GROUPS * SSM_N))],
        scratch_shapes=[pltpu.VMEM((SSM_HEADS, SSM_N, SSM_P), F32)],
        compiler_params=_cparams(dimension_semantics=("arbitrary", "arbitrary")),
    )(xs, la, xbc, xbc, st, dy)


SB_BLK = 128
SB_HPB = 4
_SB_SCALE = HEAD ** -0.5


def _mm2(x, c):
    hi, lo = _split2(x)
    if x.ndim == 3:
        c = _bcast(c, x.shape[0])
    return _dg(hi, c, 1, 0) + _dg(lo, c, 1, 0)


def _sb_tile(q, kj, diag):
    z = _dg(q, _bf(kj), 1, 1) * _SB_SCALE
    ls = -_softplus(-z)
    r, c = _iota((SB_BLK, SB_BLK), 0), _iota((SB_BLK, SB_BLK), 1)
    valid = jnp.logical_or(jnp.logical_not(diag), c < r)
    lk = ls - z
    return ls, lk, jnp.where(valid, lk, 0.0), valid


def sb_attn_fwd(proj):
    lp = proj.shape[0]
    nq = lp // SB_BLK
    wide = SB_HPB * HEAD

    def body(q_ref, k_ref, v_ref, o_ref, tot_ref):
        i = pl.program_id(1)
        q = _bf(_heads(q_ref[...], SB_HPB))
        r, c = _iota((SB_BLK, SB_BLK), 0), _iota((SB_BLK, SB_BLK), 1)
        later_mat = (r > c).astype(BF16)

        def step(t, carry):
            cs, acc = carry
            j = i - t
            rows = pl.ds(pl.multiple_of(j * SB_BLK, SB_BLK), SB_BLK)
            ls, _, lkm, valid = _sb_tile(q, _heads(k_ref[rows, :], SB_HPB), t == 0)
            w = jnp.where(valid, jnp.exp(ls + _mm2(lkm, later_mat) + cs), 0.0)
            acc = acc + _dg(_bf(w), _bf(_heads(v_ref[rows, :], SB_HPB)), 1, 0)
            return cs + jnp.sum(lkm, axis=-1, keepdims=True), acc

        cs, acc = lax.fori_loop(0, i + 1, step, (jnp.zeros((SB_HPB, SB_BLK, 1), F32),
                                                 jnp.zeros((SB_HPB, SB_BLK, HEAD), F32)))
        o_ref[...] = _unheads(acc)
        tot_ref[...] = _unheads(jnp.broadcast_to(cs, (SB_HPB, SB_BLK, HEAD)))

    out_blk = pl.BlockSpec((SB_BLK, wide), lambda h, i: (i, h))
    return pl.pallas_call(
        body, name="sb_attn_fwd", grid=(N_HEADS // SB_HPB, nq),
        in_specs=[pl.BlockSpec((SB_BLK, wide), lambda h, i: (i, C_SBQ // wide + h)),
                  pl.BlockSpec((lp, wide), lambda h, i: (0, C_SBK // wide + h)),
                  pl.BlockSpec((lp, wide), lambda h, i: (0, C_SBV // wide + h))],
        out_specs=[out_blk, out_blk],
        out_shape=[_sds((lp, N_HEADS * HEAD)), _sds((lp, N_HEADS * HEAD))],
        compiler_params=_cparams(dimension_semantics=("arbitrary", "arbitrary")),
    )(proj, proj, proj)


def sb_attn_bwd(proj, tot, do):
    lp = proj.shape[0]
    nq = lp // SB_BLK
    wide = SB_HPB * HEAD

    def body(q_ref, k_ref, v_ref, tot_ref, do_ref, dq_ref, dk_ref, dv_ref):
        i = pl.program_id(1)

        @pl.when(i == 0)
        def _():
            dk_ref[...] = jnp.zeros((lp, wide), F32)
            dv_ref[...] = jnp.zeros((lp, wide), F32)

        q = _bf(_heads(q_ref[...], SB_HPB))
        do = _bf(_heads(do_ref[...], SB_HPB))
        tot = _heads(tot_ref[...], SB_HPB)
        r, c = _iota((SB_BLK, SB_BLK), 0), _iota((SB_BLK, SB_BLK), 1)
        upto_mat = (r <= c).astype(BF16)
        before_mat = (r < c).astype(BF16)

        def step(j, carry):
            pre, gs, dq = carry
            rows = pl.ds(pl.multiple_of(j * SB_BLK, SB_BLK), SB_BLK)
            kj = _bf(_heads(k_ref[rows, :], SB_HPB))
            vj = _bf(_heads(v_ref[rows, :], SB_HPB))
            ls, lk, lkm, valid = _sb_tile(q, kj, j == i)
            w = jnp.where(valid, jnp.exp(ls + (tot - (pre + _mm2(lkm, upto_mat)))), 0.0)
            g = w * _dg(do, vj, 1, 1)
            dlk = gs + _mm2(g, before_mat)
            dz = _bf(jnp.where(valid, g * jnp.exp(lk) - dlk * jnp.exp(ls), 0.0) * _SB_SCALE)
            dk_ref[rows, :] += _unheads(_dg(dz, q, 0, 0))
            dv_ref[rows, :] += _unheads(_dg(_bf(w), do, 0, 0))
            return (pre + jnp.sum(lkm, axis=-1, keepdims=True), gs + jnp.sum(g, axis=-1, keepdims=True),
                    dq + _dg(dz, kj, 1, 0))

        zero = jnp.zeros((SB_HPB, SB_BLK, 1), F32)
        _, _, dq = lax.fori_loop(0, i + 1, step, (zero, zero, jnp.zeros((SB_HPB, SB_BLK, HEAD), F32)))
        dq_ref[...] = _unheads(dq)

    row_blk = pl.BlockSpec((SB_BLK, wide), lambda h, i: (i, h))
    full_blk = pl.BlockSpec((lp, wide), lambda h, i: (0, h))
    return pl.pallas_call(
        body, name="sb_attn_bwd", grid=(N_HEADS // SB_HPB, nq),
        in_specs=[pl.BlockSpec((SB_BLK, wide), lambda h, i: (i, C_SBQ // wide + h)),
                  pl.BlockSpec((lp, wide), lambda h, i: (0, C_SBK // wide + h)),
                  pl.BlockSpec((lp, wide), lambda h, i: (0, C_SBV // wide + h)),
                  row_blk, row_blk],
        out_specs=[row_blk, full_blk, full_blk],
        out_shape=[_sds((lp, N_HEADS * HEAD))] * 3,
        compiler_params=_cparams(dimension_semantics=("arbitrary", "arbitrary")),
    )(proj, proj, proj, tot, do)


ROW_TILE = 128


def _row_in_specs(rows, consts, tm):
    specs = [pl.BlockSpec((tm, r[2]), functools.partial(lambda i, cb, rb: (rb + i, cb), cb=r[1],
                                                         rb=r[3] if len(r) > 3 else 0)) for r in rows]
    specs += [pl.BlockSpec(c.shape, lambda i: (0, 0)) for c in consts]
    return specs


def rowwise(fn, name, rows, consts, outs, tm=ROW_TILE, n_rows=None):
    n_in = len(rows) + len(consts)
    lp = rows[0][0].shape[0] if n_rows is None else n_rows

    def body(*refs):
        res = fn(*[r[...].astype(F32) for r in refs[:n_in]])
        for o_ref, val in zip(refs[n_in:], res, strict=True):
            o_ref[...] = val.astype(o_ref.dtype)

    return pl.pallas_call(
        body, name=name, grid=(lp // tm,),
        in_specs=_row_in_specs(rows, consts, tm),
        out_specs=[pl.BlockSpec((tm, w), lambda i: (i, 0)) for w, _ in outs],
        out_shape=[_sds((lp, w), dt) for w, dt in outs],
        compiler_params=_cparams(dimension_semantics=("arbitrary",)),
    )(*[r[0] for r in rows], *consts)


def rowwise_vjp(fn, name, rows, consts, cots, d_dtypes, tm=ROW_TILE):
    n_r, n_c, n_o = len(rows), len(consts), len(cots)
    lp = rows[0][0].shape[0]
    wanted = [k for k, dt in enumerate(d_dtypes) if dt is not None]

    def body(*refs):
        ins = [r[...].astype(F32) for r in refs[:n_r + n_c]]
        cot = tuple(r[...].astype(F32) for r in refs[n_r + n_c:n_r + n_c + n_o])
        out_refs = refs[n_r + n_c + n_o:]
        _, vjp = jax.vjp(fn, *ins)
        grads = vjp(cot)
        for o_ref, k in zip(out_refs[:len(wanted)], wanted):
            o_ref[...] = grads[k].astype(o_ref.dtype)

        @pl.when(pl.program_id(0) == 0)
        def _():
            for o_ref in out_refs[len(wanted):]:
                o_ref[...] = jnp.zeros(o_ref.shape, F32)

        for o_ref, g in zip(out_refs[len(wanted):], grads[n_r:], strict=True):
            o_ref[...] += g

    return pl.pallas_call(
        body, name=name, grid=(lp // tm,),
        in_specs=_row_in_specs(rows, consts, tm) + [pl.BlockSpec((tm, c.shape[1]), lambda i: (i, 0)) for c in cots],
        out_specs=[pl.BlockSpec((tm, rows[k][2]), lambda i: (i, 0)) for k in wanted]
        + [pl.BlockSpec(c.shape, lambda i: (0, 0)) for c in consts],
        out_shape=[_sds((lp, rows[k][2]), d_dtypes[k]) for k in wanted] + [_sds(c.shape) for c in consts],
        compiler_params=_cparams(dimension_semantics=("arbitrary",)),
    )(*[r[0] for r in rows], *consts, *cots)


def _rms(t, eps):
    return t * lax.rsqrt(jnp.mean(t * t, axis=-1, keepdims=True) + eps)


def f_rmsnorm(h, g):
    return (_rms(h, RMS_EPS) * g,)


def _per_head(t, fn):
    return jnp.concatenate([fn(t[:, h * HEAD:(h + 1) * HEAD]) for h in range(t.shape[1] // HEAD)], axis=1)


_expand_beta = make_expand(0, HEAD, N_HEADS * HEAD)
_expand_g = make_expand(8, HEAD, N_HEADS * HEAD)
_expand_dt = make_expand(16, SSM_P, SSM_INNER)
_expand_d = make_expand(0, SSM_P, SSM_INNER)


def f_prep(cq, ck, x, misc, bias_row, alog_row):
    def l2(t):
        return t * lax.rsqrt(jnp.sum(t * t, axis=-1, keepdims=True) + L2_EPS)

    qh = _per_head(cq, l2) * (HEAD ** -0.5)
    kh = _per_head(ck, l2)
    step = _softplus(misc + bias_row)
    decay = -jnp.exp(alog_row) * step
    bx = _expand_beta(_sigmoid(misc))
    gx = _expand_g(decay)
    xs = x * _expand_dt(step)
    la = _expand_dt(decay)
    return qh, kh, bx, gx, xs, la


def f_gate(o_a, sb_z, o_b, gd_z, y, x, ss_z, gdn_g, ssm_g, d_row):
    a_in = o_a * _silu(sb_z)
    b_in = _per_head(o_b, lambda t: _rms(t, RMS_EPS) * gdn_g) * _silu(gd_z)
    y2 = (y + _expand_d(d_row) * x) * _silu(ss_z)
    half = SSM_INNER // SSM_GROUPS
    c_in = jnp.concatenate([_rms(y2[:, g * half:(g + 1) * half], RMS_EPS) * ssm_g[:, g * half:(g + 1) * half]
                            for g in range(SSM_GROUPS)], axis=1)
    return a_in, b_in, c_in


def f_mix(pa, pb, pc, ga, gb, gc):
    return (_sigmoid(ga) * pa + _sigmoid(gb) * pb + _sigmoid(gc) * pc,)


def f_add(a, b):
    return (a + b,)


def final_loss(h, tgt, mask, g):
    lp = h.shape[0]
    tm = ROW_TILE

    def body(h_ref, t_ref, m_ref, g_ref, loss_ref, dh_ref, dg_ref):
        tgt, msk = t_ref[...], m_ref[:, :1]

        def f(h, g):
            err = _rms(h, RMS_EPS) * g - tgt
            return jnp.sum(0.5 * jnp.mean(err * err, axis=-1, keepdims=True) * msk, axis=0, keepdims=True)

        val, vjp = jax.vjp(f, h_ref[...], g_ref[...])
        dh, dg = vjp(jnp.ones((1, 1), F32))
        dh_ref[...] = dh

        @pl.when(pl.program_id(0) == 0)
        def _():
            loss_ref[...] = jnp.zeros(loss_ref.shape, F32)
            dg_ref[...] = jnp.zeros(dg_ref.shape, F32)

        loss_ref[...] += jnp.broadcast_to(val, loss_ref.shape)
        dg_ref[...] += dg

    d = h.shape[1]
    return pl.pallas_call(
        body, name="final_loss", grid=(lp // tm,),
        in_specs=[pl.BlockSpec((tm, d), lambda i: (i, 0)), pl.BlockSpec((tm, d), lambda i: (i, 0)),
                  pl.BlockSpec((tm, LANE), lambda i: (i, 0)), pl.BlockSpec((1, d), lambda i: (0, 0))],
        out_specs=[pl.BlockSpec((1, LANE), lambda i: (0, 0)), pl.BlockSpec((tm, d), lambda i: (i, 0)),
                   pl.BlockSpec((1, d), lambda i: (0, 0))],
        out_shape=[_sds((1, LANE)), _sds((lp, d)), _sds((1, d))],
        compiler_params=_cparams(dimension_semantics=("arbitrary",)),
    )(h, tgt, mask, g)


CONV_TILE = 512


def _shift_down(x, s):
    if s == 0:
        return x
    return jnp.where(_iota(x.shape, 0) >= s, pltpu.roll(x, s, 0), 0.0)


def _shift_up(x, s):
    if s == 0:
        return x
    n = x.shape[0]
    return jnp.where(_iota(x.shape, 0) < n - s, pltpu.roll(x, n - s, 0), 0.0)


def _conv_pre(x, w_ref, b):
    pre = b + w_ref[CONV_K - 1:CONV_K, :] * x
    for k in range(CONV_K - 1):
        pre = pre + w_ref[k:k + 1, :] * _shift_down(x, CONV_K - 1 - k)
    return pre


def conv_silu_fwd(x, col, width, w, b, name):
    lp = x.shape[0]
    ct = CONV_TILE if width % CONV_TILE == 0 else LANE * 2

    def body(x_ref, w_ref, b_ref, y_ref):
        y_ref[...] = _silu(_conv_pre(x_ref[...], w_ref, b_ref[...]))

    return pl.pallas_call(
        body, name=name, grid=(width // ct,),
        in_specs=[pl.BlockSpec((lp, ct), lambda j: (0, col // ct + j)),
                  pl.BlockSpec((CONV_K, ct), lambda j: (0, j)), pl.BlockSpec((1, ct), lambda j: (0, j))],
        out_specs=pl.BlockSpec((lp, ct), lambda j: (0, j)),
        out_shape=_sds((lp, width)),
        compiler_params=_cparams(dimension_semantics=("arbitrary",)),
    )(x, w, b)


def conv_silu_bwd(x, col, width, w, b, dy, name):
    lp = x.shape[0]
    ct = CONV_TILE if width % CONV_TILE == 0 else LANE * 2

    def body(x_ref, w_ref, b_ref, dy_ref, dx_ref, dw_ref, db_ref):
        x = x_ref[...]
        pre = _conv_pre(x, w_ref, b_ref[...])
        sg = _sigmoid(pre)
        dpre = dy_ref[...] * (sg * (1.0 + pre * (1.0 - sg)))
        dx = w_ref[CONV_K - 1:CONV_K, :] * dpre
        for k in range(CONV_K - 1):
            dx = dx + w_ref[k:k + 1, :] * _shift_up(dpre, CONV_K - 1 - k)
        dx_ref[...] = dx.astype(dx_ref.dtype)
        for k in range(CONV_K):
            dw_ref[k:k + 1, :] = jnp.sum(dpre * _shift_down(x, CONV_K - 1 - k), axis=0, keepdims=True)
        db_ref[...] = jnp.sum(dpre, axis=0, keepdims=True)

    return pl.pallas_call(
        body, name=name, grid=(width // ct,),
        in_specs=[pl.BlockSpec((lp, ct), lambda j: (0, col // ct + j)),
                  pl.BlockSpec((CONV_K, ct), lambda j: (0, j)), pl.BlockSpec((1, ct), lambda j: (0, j)),
                  pl.BlockSpec((lp, ct), lambda j: (0, j))],
        out_specs=[pl.BlockSpec((lp, ct), lambda j: (0, j)), pl.BlockSpec((CONV_K, ct), lambda j: (0, j)),
                   pl.BlockSpec((1, ct), lambda j: (0, j))],
        out_shape=[_sds((lp, width), BF16), _sds((CONV_K, width)), _sds((1, width))],
        compiler_params=_cparams(dimension_semantics=("arbitrary",)),
    )(x, w, b, dy)


def matmul(a, b, out_dtype, tn, name, add=None, tk=None):
    m, kdim = a.shape
    n = b.shape[1]
    if tk is None:
        def body(a_ref, b_ref, *rest):
            acc = _dg(_bf(a_ref[...]), _bf(b_ref[...]), 1, 0)
            if add is not None:
                acc = acc + rest[0][...]
            rest[-1][...] = acc.astype(out_dtype)

        in_specs = [pl.BlockSpec((m, kdim), lambda j: (0, 0)), pl.BlockSpec((kdim, tn), lambda j: (0, j))]
        if add is not None:
            in_specs.append(pl.BlockSpec((m, tn), lambda j: (0, j)))
        return pl.pallas_call(
            body, name=name, grid=(n // tn,), in_specs=in_specs,
            out_specs=pl.BlockSpec((m, tn), lambda j: (0, j)), out_shape=_sds((m, n), out_dtype),
            compiler_params=_cparams(dimension_semantics=("arbitrary",)),
        )(a, b, *([] if add is None else [add]))

    nk = kdim // tk

    def body_k(a_ref, b_ref, *rest):
        o_ref, acc_ref = rest[-2], rest[-1]
        k = pl.program_id(1)

        @pl.when(k == 0)
        def _():
            acc_ref[...] = jnp.zeros(acc_ref.shape, F32)

        acc_ref[...] += _dg(_bf(a_ref[...]), _bf(b_ref[...]), 1, 0)

        @pl.when(k == nk - 1)
        def _():
            acc = acc_ref[...]
            if add is not None:
                acc = acc + rest[0][...]
            o_ref[...] = acc.astype(out_dtype)

    in_specs = [pl.BlockSpec((m, tk), lambda j, k: (0, k)), pl.BlockSpec((tk, tn), lambda j, k: (k, j))]
    if add is not None:
        in_specs.append(pl.BlockSpec((m, tn), lambda j, k: (0, j)))
    return pl.pallas_call(
        body_k, name=name, grid=(n // tn, nk), in_specs=in_specs,
        out_specs=pl.BlockSpec((m, tn), lambda j, k: (0, j)), out_shape=_sds((m, n), out_dtype),
        scratch_shapes=[pltpu.VMEM((m, tn), F32)],
        compiler_params=_cparams(dimension_semantics=("arbitrary", "arbitrary")),
    )(a, b, *([] if add is None else [add]))


def f_adamw(w, g, m, v):
    m = ADAM_B1 * m + (1.0 - ADAM_B1) * g
    v = ADAM_B2 * v + (1.0 - ADAM_B2) * (g * g)
    m_hat = m / (1.0 - ADAM_B1 ** ADAM_STEP)
    v_hat = v / (1.0 - ADAM_B2 ** ADAM_STEP)
    delta = -ADAM_LR * (m_hat / (jnp.sqrt(v_hat) + ADAM_EPS) + ADAM_WD * w)
    return delta, m, v


def adamw(w, g, m, v, name):
    shape = w.shape
    cols = shape[-1]
    rows = math.prod(shape[:-1]) if len(shape) > 1 else 1
    tm = next((t for t in (256, 128, 64, 32, 16, 8) if rows % t == 0), rows)
    args = [(t.reshape(rows, cols), 0, cols) for t in (w, g, m, v)]
    outs = rowwise(f_adamw, name, args, [], [(cols, F32)] * 3, tm=tm)
    return [o.reshape(shape) for o in outs]


_MESH = pl.DeviceIdType.MESH
_HBM = pl.BlockSpec(memory_space=pltpu.HBM)


def _my_pos():
    return lax.axis_index("x"), lax.axis_index("y"), lax.axis_index("c")


def _linear(px, py, pc):
    return 4 * px + 2 * py + pc


def all_gather(x, name):
    def body(x_ref, out_ref, send_sems, recv_sems, local_sem):
        x, y, c = _my_pos()
        me, sibling = (x, y, c), (x, y, 1 - c)
        chips = [(1 - x, y), (x, 1 - y), (1 - x, 1 - y)]

        def slab(px, py, pc):
            return out_ref.at[_linear(px, py, pc)]

        def copy(k, block, to, src=None):
            return pltpu.make_async_remote_copy(
                src_ref=slab(*block) if src is None else src, dst_ref=slab(*block),
                send_sem=send_sems.at[k], recv_sem=recv_sems.at[k], device_id=to, device_id_type=_MESH)

        mine = pltpu.make_async_copy(x_ref, slab(*me), local_sem)
        mine.start()
        first = [copy(0, me, sibling, src=x_ref)]
        first += [copy(1 + j, me, (*chip, c), src=x_ref) for j, chip in enumerate(chips)]
        for cp in first:
            cp.start()
        passed = [copy(4 + j, (*chip, c), sibling) for j, chip in enumerate(chips)]
        for j, chip in enumerate(chips):
            copy(1 + j, (*chip, c), me).wait_recv()
            passed[j].start()
        copy(0, sibling, me).wait_recv()
        for j, chip in enumerate(chips):
            copy(4 + j, (*chip, 1 - c), me).wait_recv()
        for cp in first + passed:
            cp.wait_send()
        mine.wait()

    return pl.pallas_call(
        body, name=name, out_shape=_sds((N_DEV,) + x.shape, x.dtype), in_specs=[_HBM], out_specs=_HBM,
        scratch_shapes=[pltpu.SemaphoreType.DMA((7,)), pltpu.SemaphoreType.DMA((7,)), pltpu.SemaphoreType.DMA],
    )(x)


def scatter_blocks(g, name):
    def body(g_ref, out_ref, send_sems, recv_sems):
        x, y, c = _my_pos()
        copies = []
        for r in range(1, N_DEV):
            peer = (1 - x if r & 4 else x, 1 - y if r & 2 else y, 1 - c if r & 1 else c)
            copies.append(pltpu.make_async_remote_copy(
                src_ref=g_ref.at[_linear(*peer)], dst_ref=out_ref.at[r - 1],
                send_sem=send_sems.at[r - 1], recv_sem=recv_sems.at[r - 1], device_id=peer, device_id_type=_MESH))
        for cp in copies:
            cp.start()
        for cp in copies:
            cp.wait()

    return pl.pallas_call(
        body, name=name, out_shape=_sds((N_DEV - 1,) + g.shape[1:], g.dtype), in_specs=[_HBM], out_specs=_HBM,
        scratch_shapes=[pltpu.SemaphoreType.DMA((7,)), pltpu.SemaphoreType.DMA((7,))],
    )(g)


def f_sum8(*xs):
    acc = xs[0]
    for t in xs[1:]:
        acc = acc + t
    return (acc,)


W_IN_SHARD = D_IN // N_DEV
PACK_ROWS = 2688
_PACK_SPLITS = (W_IN_SHARD, 128, 128, 256, 128)


def pack_layer(w_in, wa, wb, wc, wo):
    parts = [w_in.reshape(W_IN_SHARD, D_MODEL), wa, wb, wc, wo]
    rows = sum(_PACK_SPLITS)
    return jnp.concatenate(parts + [jnp.zeros((PACK_ROWS - rows, D_MODEL), w_in.dtype)], axis=0)


def unpack_layer(p):
    out, r = [], 0
    for n in _PACK_SPLITS:
        out.append(p[r:r + n])
        r += n
    out[0] = out[0].reshape(D_MODEL, W_IN_SHARD)
    return out


def to_kernel_columns(w):
    k = w.shape[0]
    out = jnp.zeros((k, D_INC), w.dtype)
    for src, width, dst in _SEGS:
        out = lax.dynamic_update_slice(out, w[:, src:src + width], (0, dst))
    return out


def from_kernel_columns(w):
    parts = sorted(_SEGS)
    return jnp.concatenate([w[:, dst:dst + width] for _, width, dst in parts], axis=1)


def _lane_row(pieces):
    row = jnp.zeros((1, LANE), F32)
    for first, vec in pieces:
        row = lax.dynamic_update_slice(row, vec.reshape(1, -1).astype(F32), (0, first))
    return row


_D = D_MODEL


def layer_fwd(h, p):
    (u,) = rowwise(f_rmsnorm, "rmsnorm_fwd", [(h, 0, _D)], [p["norm_g"]], [(_D, BF16)])
    proj = matmul(u, p["w_in"], F32, 640, "in_proj")
    cqkv = conv_silu_fwd(proj, C_GDQ, GDN_CONV_CH, p["gdn_conv_w"], jnp.zeros((1, GDN_CONV_CH), F32), "gdn_conv_fwd")
    xbc = conv_silu_fwd(proj, C_XBC, SSM_CONV_CH, p["ssm_conv_w"], p["ssm_conv_b"], "ssm_conv_fwd")
    prep_rows = [(cqkv, 0, 1024), (cqkv, 1, 1024), (xbc, 0, SSM_INNER), (proj, C_MISC // LANE, LANE)]
    prep_consts = [p["bias_row"], p["alog_row"]]
    qh, kh, bx, gx, xs, la = rowwise(f_prep, "prep_fwd", prep_rows, prep_consts,
                                     [(1024, F32)] * 4 + [(SSM_INNER, F32)] * 2)
    o_a, tot = sb_attn_fwd(proj)
    o_b, st_g = gdn_scan_fwd(qh, kh, cqkv, bx, gx)
    y, st_s = ssd_scan_fwd(xs, la, xbc)
    gate_rows = [(o_a, 0, 1024), (proj, C_SBZ // 1024, 1024), (o_b, 0, 1024), (proj, C_GDZ // 1024, 1024),
                 (y, 0, SSM_INNER), (xbc, 0, SSM_INNER), (proj, C_SSZ // SSM_INNER, SSM_INNER)]
    gate_consts = [p["gdn_norm_g"], p["ssm_norm_g"], p["d_row"]]
    a_in, b_in, c_in = rowwise(f_gate, "gate_fwd", gate_rows, gate_consts,
                               [(1024, BF16), (1024, BF16), (SSM_INNER, BF16)])
    pa = matmul(a_in, p["wa"], F32, 512, "branch_a")
    pb = matmul(b_in, p["wb"], F32, 512, "branch_b")
    pc = matmul(c_in, p["wc"], F32, 512, "branch_c")
    mix_rows = [(pa, 0, _D), (pb, 0, _D), (pc, 0, _D)] + [(proj, C_GATE // _D + k, _D) for k in range(3)]
    (merged,) = rowwise(f_mix, "mix_fwd", mix_rows, [], [(_D, BF16)])
    h_new = matmul(merged, p["wo"], F32, 512, "out_proj", add=h)
    saved = dict(h=h, u=u, proj=proj, cqkv=cqkv, xbc=xbc, qh=qh, kh=kh, bx=bx, gx=gx, xs=xs, la=la, o_a=o_a, tot=tot,
                 o_b=o_b, st_g=st_g, y=y, st_s=st_s, a_in=a_in, b_in=b_in, c_in=c_in, pa=pa, pb=pb, pc=pc,
                 merged=merged, prep_rows=prep_rows, prep_consts=prep_consts, gate_rows=gate_rows,
                 gate_consts=gate_consts, mix_rows=mix_rows)
    return h_new, saved


def layer_bwd(dh, s, p):
    proj = s["proj"]
    dmerged = matmul(dh, p["wo_t"], F32, 512, "d_merged")
    g = {"wo": matmul(s["merged"].T, dh, F32, 512, "dw_out")}
    dpa, dpb, dpc, dga, dgb, dgc = rowwise_vjp(f_mix, "mix_bwd", s["mix_rows"], [], [dmerged], [BF16] * 6)
    da_in = matmul(dpa, p["wa_t"], F32, 512, "d_branch_a")
    db_in = matmul(dpb, p["wb_t"], F32, 512, "d_branch_b")
    dc_in = matmul(dpc, p["wc_t"], F32, 512, "d_branch_c")
    g["wa"] = matmul(s["a_in"].T, dpa, F32, 512, "dw_branch_a")
    g["wb"] = matmul(s["b_in"].T, dpb, F32, 512, "dw_branch_b")
    g["wc"] = matmul(s["c_in"].T, dpc, F32, 512, "dw_branch_c")
    (d_oa, d_sbz, d_ob, d_gdz, dy, dx_gate, d_ssz, g["gdn_norm_g"], g["ssm_norm_g"], g["d_row"]) = rowwise_vjp(
        f_gate, "gate_bwd", s["gate_rows"], s["gate_consts"], [da_in, db_in, dc_in],
        [F32, BF16, F32, BF16, F32, F32, BF16])
    dq_sb, dk_sb, dv_sb = sb_attn_bwd(proj, s["tot"], d_oa)
    dqh, dkh, dv_g, dbx, dgx = gdn_scan_bwd(s["qh"], s["kh"], s["cqkv"], s["bx"], s["gx"], s["st_g"], d_ob)
    dxs, dla, db_s, dc_s = ssd_scan_bwd(s["xs"], s["la"], s["xbc"], s["st_s"], dy)
    dcq, dck, dx_prep, dmisc, g["bias_row"], g["alog_row"] = rowwise_vjp(
        f_prep, "prep_bwd", s["prep_rows"], s["prep_consts"], [dqh, dkh, dbx, dgx, dxs, dla], [F32, F32, F32, BF16])
    d_cqkv = jnp.concatenate([dcq, dck, dv_g], axis=1)
    din_g, g["gdn_conv_w"], _ = conv_silu_bwd(proj, C_GDQ, GDN_CONV_CH, p["gdn_conv_w"],
                                              jnp.zeros((1, GDN_CONV_CH), F32), d_cqkv, "gdn_conv_bwd")
    (dx,) = rowwise(f_add, "add_dx", [(dx_gate, 0, SSM_INNER), (dx_prep, 0, SSM_INNER)], [], [(SSM_INNER, F32)])
    d_xbc = jnp.concatenate([dx, db_s, dc_s], axis=1)
    din_s, g["ssm_conv_w"], g["ssm_conv_b"] = conv_silu_bwd(proj, C_XBC, SSM_CONV_CH, p["ssm_conv_w"],
                                                            p["ssm_conv_b"], d_xbc, "ssm_conv_bwd")
    pad = jnp.zeros((proj.shape[0], D_INC - C_MISC - LANE), BF16)
    dproj = jnp.concatenate([_bf(dq_sb), _bf(dk_sb), _bf(dv_sb), d_sbz, din_g, d_gdz, d_ssz, dga, dgb, dgc, din_s,
                             dmisc] + ([pad] if pad.shape[1] else []), axis=1)
    du = matmul(dproj, p["w_in_t"], F32, _D, "d_u", tk=640)
    g["w_in"] = matmul(s["u"].T, dproj, F32, 640, "dw_in")
    dh_norm, g["norm_g"] = rowwise_vjp(f_rmsnorm, "rmsnorm_bwd", [(s["h"], 0, _D)], [p["norm_g"]], [du], [F32])
    (dh_in,) = rowwise(f_add, "add_dh", [(dh, 0, _D), (dh_norm, 0, _D)], [], [(_D, F32)])
    return dh_in, g


_REPLICATED = ("norm_g", "gdn_a_log", "gdn_dt_bias", "gdn_norm_g", "ssm_conv_b", "ssm_a_log", "ssm_dt_bias", "ssm_d",
               "ssm_norm_g", "final_norm_g")
_WEIGHTS = ("meta_tokens", "norm_g", "w_in", "gdn_conv_w", "gdn_a_log", "gdn_dt_bias", "gdn_norm_g", "ssm_conv_w",
            "ssm_conv_b", "ssm_a_log", "ssm_dt_bias", "ssm_d", "ssm_norm_g", "w_branch_a", "w_branch_b", "w_branch_c",
            "w_out", "final_norm_g")
SMALL_ROWS = 128


def _flatten_small(parts):
    flat = jnp.concatenate([t.reshape(-1).astype(F32) for t in parts])
    return jnp.pad(flat, (0, SMALL_ROWS * _D - flat.shape[0])).reshape(SMALL_ROWS, _D)


def _unflatten_small(block, shapes):
    flat, out, at = block.reshape(-1), [], 0
    for shp in shapes:
        n = math.prod(shp)
        out.append(flat[at:at + n].reshape(shp))
        at += n
    return out


def kernel(x, meta_tokens, norm_g, w_in, gdn_conv_w, gdn_a_log, gdn_dt_bias, gdn_norm_g, ssm_conv_w, ssm_conv_b, ssm_a_log, ssm_dt_bias, ssm_d, ssm_norm_g, w_branch_a, w_branch_b, w_branch_c, w_out, final_norm_g, loss_target, m_meta_tokens, m_norm_g, m_w_in, m_gdn_conv_w, m_gdn_a_log, m_gdn_dt_bias, m_gdn_norm_g, m_ssm_conv_w, m_ssm_conv_b, m_ssm_a_log, m_ssm_dt_bias, m_ssm_d, m_ssm_norm_g, m_w_branch_a, m_w_branch_b, m_w_branch_c, m_w_out, m_final_norm_g, v_meta_tokens, v_norm_g, v_w_in, v_gdn_conv_w, v_gdn_a_log, v_gdn_dt_bias, v_gdn_norm_g, v_ssm_conv_w, v_ssm_conv_b, v_ssm_a_log, v_ssm_dt_bias, v_ssm_d, v_ssm_norm_g, v_w_branch_a, v_w_branch_b, v_w_branch_c, v_w_out, v_final_norm_g):
    weights = dict(meta_tokens=meta_tokens, norm_g=norm_g, w_in=w_in, gdn_conv_w=gdn_conv_w, gdn_a_log=gdn_a_log,
                   gdn_dt_bias=gdn_dt_bias, gdn_norm_g=gdn_norm_g, ssm_conv_w=ssm_conv_w, ssm_conv_b=ssm_conv_b,
                   ssm_a_log=ssm_a_log, ssm_dt_bias=ssm_dt_bias, ssm_d=ssm_d, ssm_norm_g=ssm_norm_g,
                   w_branch_a=w_branch_a, w_branch_b=w_branch_b, w_branch_c=w_branch_c, w_out=w_out,
                   final_norm_g=final_norm_g)
    m_in = dict(zip(_WEIGHTS, (m_meta_tokens, m_norm_g, m_w_in, m_gdn_conv_w, m_gdn_a_log, m_gdn_dt_bias, m_gdn_norm_g,
                               m_ssm_conv_w, m_ssm_conv_b, m_ssm_a_log, m_ssm_dt_bias, m_ssm_d, m_ssm_norm_g,
                               m_w_branch_a, m_w_branch_b, m_w_branch_c, m_w_out, m_final_norm_g)))
    v_in = dict(zip(_WEIGHTS, (v_meta_tokens, v_norm_g, v_w_in, v_gdn_conv_w, v_gdn_a_log, v_gdn_dt_bias, v_gdn_norm_g,
                               v_ssm_conv_w, v_ssm_conv_b, v_ssm_a_log, v_ssm_dt_bias, v_ssm_d, v_ssm_norm_g,
                               v_w_branch_a, v_w_branch_b, v_w_branch_c, v_w_out, v_final_norm_g)))
    depth = w_in.shape[0]
    seq = x.shape[1]
    tokens = N_META + seq
    lp = -(-tokens // SB_BLK) * SB_BLK
    me = _linear(*_my_pos())

    packed = jnp.concatenate([pack_layer(_bf(w_in[l]), _bf(w_branch_a[l]), _bf(w_branch_b[l]), _bf(w_branch_c[l]),
                                         _bf(w_out[l])) for l in range(depth)], axis=0)
    big = all_gather(packed, "gather_weights").reshape(N_DEV, depth, PACK_ROWS, _D)
    small_in = jnp.concatenate([gdn_conv_w.reshape(-1), ssm_conv_w.reshape(-1), meta_tokens.reshape(-1)])
    small_in = small_in.reshape(-1, LANE)
    small = all_gather(small_in, "gather_small").reshape(N_DEV, -1)
    n_g, n_s = gdn_conv_w.size, ssm_conv_w.size
    gdn_w_full = small[:, :n_g].reshape((N_DEV,) + gdn_conv_w.shape).transpose(1, 2, 0, 3).reshape(depth, CONV_K, -1)
    ssm_w_full = small[:, n_g:n_g + n_s].reshape((N_DEV,) + ssm_conv_w.shape).transpose(1, 2, 0, 3)
    ssm_w_full = ssm_w_full.reshape(depth, CONV_K, -1)
    meta_full = small[:, n_g + n_s:].reshape(N_DEV, N_META, -1).transpose(1, 0, 2).reshape(N_META, _D)

    params = []
    for l in range(depth):
        blocks = [unpack_layer(big[d, l]) for d in range(N_DEV)]
        w_full = jnp.concatenate([b[0] for b in blocks], axis=1)
        wk = to_kernel_columns(w_full)
        wa, wb, wc, wo = (jnp.concatenate([b[k] for b in blocks], axis=0) for k in range(1, 5))
        params.append(dict(
            norm_g=norm_g[l].reshape(1, _D), w_in=wk, w_in_t=wk.T, gdn_conv_w=gdn_w_full[l], ssm_conv_w=ssm_w_full[l],
            ssm_conv_b=ssm_conv_b[l].reshape(1, -1),
            bias_row=_lane_row([(8, gdn_dt_bias[l]), (16, ssm_dt_bias[l])]),
            alog_row=_lane_row([(8, gdn_a_log[l]), (16, ssm_a_log[l])]),
            gdn_norm_g=gdn_norm_g[l].reshape(1, HEAD), ssm_norm_g=ssm_norm_g[l].reshape(1, SSM_INNER),
            d_row=_lane_row([(0, ssm_d[l])]), wa=wa, wb=wb, wc=wc, wo=wo, wa_t=wa.T, wb_t=wb.T, wc_t=wc.T, wo_t=wo.T))

    h = jnp.concatenate([meta_full, x[0], jnp.zeros((lp - tokens, _D), F32)], axis=0)
    tgt = jnp.pad(loss_target[0], ((N_META, lp - tokens), (0, 0)))
    rows = lax.broadcasted_iota(jnp.int32, (lp, LANE), 0)
    mask = jnp.logical_and(rows >= N_META, rows < tokens).astype(F32)

    saved = []
    for l in range(depth):
        h, s = layer_fwd(h, params[l])
        saved.append(s)
    loss_row, dh, d_final_g = final_loss(h, tgt, mask, final_norm_g.reshape(1, _D))
    grads = [None] * depth
    for l in reversed(range(depth)):
        dh, grads[l] = layer_bwd(dh, saved[l], params[l])

    send = []
    gw_full = [from_kernel_columns(grads[l]["w_in"]) for l in range(depth)]
    for d in range(N_DEV):
        per_layer = []
        for l in range(depth):
            g = grads[l]
            gw = gw_full[l][:, d * W_IN_SHARD:(d + 1) * W_IN_SHARD]
            per_layer.append(pack_layer(gw, g["wa"][d * 128:(d + 1) * 128], g["wb"][d * 128:(d + 1) * 128],
                                        g["wc"][d * 256:(d + 1) * 256], g["wo"][d * 128:(d + 1) * 128]))
        send.append(jnp.concatenate(per_layer, axis=0))
    send = jnp.stack(send)
    own = lax.dynamic_index_in_dim(send, me, axis=0, keepdims=False)
    got = scatter_blocks(_bf(send), "scatter_grads")
    n_rows = depth * PACK_ROWS
    got2 = got.reshape((N_DEV - 1) * n_rows, _D)
    sum_rows = [(own, 0, _D)] + [(got2, 0, _D, r * (n_rows // ROW_TILE)) for r in range(N_DEV - 1)]
    (gsum,) = rowwise(f_sum8, "sum_grads", sum_rows, [], [(_D, F32)], n_rows=n_rows)
    gsum = gsum.reshape(depth, PACK_ROWS, _D)
    big_g = [unpack_layer(gsum[l]) for l in range(depth)]
    grad = dict(w_in=jnp.stack([b[0] for b in big_g]), w_branch_a=jnp.stack([b[1] for b in big_g]),
                w_branch_b=jnp.stack([b[2] for b in big_g]), w_branch_c=jnp.stack([b[3] for b in big_g]),
                w_out=jnp.stack([b[4] for b in big_g]))

    def stack(name, pick=lambda t: t):
        return jnp.stack([pick(grads[l][name]) for l in range(depth)])

    small_parts = [
        stack("norm_g"), stack("alog_row", lambda t: t[0, 8:16]), stack("bias_row", lambda t: t[0, 8:16]),
        stack("gdn_norm_g"), stack("ssm_conv_b"), stack("alog_row", lambda t: t[0, 16:48]),
        stack("bias_row", lambda t: t[0, 16:48]), stack("d_row", lambda t: t[0, :SSM_HEADS]), stack("ssm_norm_g"),
        d_final_g, stack("gdn_conv_w"), stack("ssm_conv_w"), dh[:N_META], loss_row[0, :1]]
    small_shapes = [(depth, _D), (depth, 8), (depth, 8), (depth, HEAD), (depth, SSM_CONV_CH), (depth, SSM_HEADS),
                    (depth, SSM_HEADS), (depth, SSM_HEADS), (depth, SSM_INNER), (_D,), (depth, CONV_K, GDN_CONV_CH),
                    (depth, CONV_K, SSM_CONV_CH), (N_META, _D), ()]
    small_all = all_gather(_flatten_small(small_parts), "gather_small_grads")
    (small_sum,) = rowwise(f_sum8, "sum_small", [(small_all.reshape(N_DEV * SMALL_ROWS, _D), 0, _D, d)
                                                 for d in range(N_DEV)], [], [(_D, F32)], n_rows=SMALL_ROWS)
    (grad["norm_g"], grad["gdn_a_log"], grad["gdn_dt_bias"], grad["gdn_norm_g"], grad["ssm_conv_b"],
     grad["ssm_a_log"], grad["ssm_dt_bias"], grad["ssm_d"], grad["ssm_norm_g"], grad["final_norm_g"], gdn_cw, ssm_cw,
     meta_g, loss) = _unflatten_small(small_sum, small_shapes)
    grad["gdn_conv_w"] = lax.dynamic_slice_in_dim(gdn_cw, me * gdn_conv_w.shape[2], gdn_conv_w.shape[2], axis=2)
    grad["ssm_conv_w"] = lax.dynamic_slice_in_dim(ssm_cw, me * ssm_conv_w.shape[2], ssm_conv_w.shape[2], axis=2)
    grad["meta_tokens"] = lax.dynamic_slice_in_dim(meta_g, me * meta_tokens.shape[1], meta_tokens.shape[1], axis=1)

    delta, new_m, new_v = {}, {}, {}
    for name in _WEIGHTS:
        delta[name], new_m[name], new_v[name] = adamw(weights[name], grad[name], m_in[name], v_in[name],
                                                      "adamw_" + name)
    grad_x = dh[N_META:tokens][None]
    return (loss, grad_x, *[grad[n] for n in _WEIGHTS], *[delta[n] for n in _WEIGHTS],
            *[new_m[n] for n in _WEIGHTS], *[new_v[n] for n in _WEIGHTS])
```

```python
import functools
import math

import jax
import jax.numpy as jnp
from jax import lax
from jax.experimental import pallas as pl
from jax.experimental.pallas import tpu as pltpu

F32 = jnp.float32
BF16 = jnp.bfloat16

N_DEV = 8
N_META = 16
D_MODEL = 1024
DEPTH = 4
RMS_EPS = 1e-6
L2_EPS = 1e-6
CONV_K = 4
HEAD = 128
N_HEADS = 8
CHUNK = 64
SSM_INNER = 2048
SSM_P = 64
SSM_HEADS = 32
SSM_N = 128
SSM_GROUPS = 2
GDN_CONV_CH = 3072
SSM_CONV_CH = 2560
LANE = 128
VMEM_LIMIT = 56 * 1024 * 1024

ADAM_LR = 0.001
ADAM_B1 = 0.9
ADAM_B2 = 0.999
ADAM_EPS = 1e-08
ADAM_WD = 0.01
ADAM_STEP = 10

D_IN = 15920
D_INC = 16000
C_SBQ, C_SBK, C_SBV, C_SBZ = 0, 1024, 2048, 3072
C_GDQ, C_GDK, C_GDV, C_GDZ = 4096, 5120, 6144, 7168
C_SSZ = 8192
C_GATE = 10240
C_XBC = 13312
C_MISC = 15872
_SEGS = (
    (0, 8192, 0),
    (8192, 16, C_MISC),
    (8208, 2048, C_SSZ),
    (10256, 2560, C_XBC),
    (12816, 32, C_MISC + 16),
    (12848, 3072, C_GATE),
)


def _cparams(**kw):
    return pltpu.CompilerParams(vmem_limit_bytes=VMEM_LIMIT, **kw)


def _bf(x):
    return x.astype(BF16)


def _dg(a, b, ca, cb):
    if a.ndim == 3:
        return lax.dot_general(a, b, (((ca + 1,), (cb + 1,)), ((0,), (0,))), preferred_element_type=F32)
    return lax.dot_general(a, b, (((ca,), (cb,)), ((), ())), preferred_element_type=F32)


@jax.custom_vjp
def mm_nn(a, b):
    return _dg(_bf(a), _bf(b), 1, 0)


def _mm_nn_fwd(a, b):
    a, b = _bf(a), _bf(b)
    return _dg(a, b, 1, 0), (a, b)


def _mm_nn_bwd(res, g):
    a, b = res
    g = _bf(g)
    return _dg(g, b, 1, 1), _dg(a, g, 0, 0)


mm_nn.defvjp(_mm_nn_fwd, _mm_nn_bwd)


@jax.custom_vjp
def mm_nt(a, b):
    return _dg(_bf(a), _bf(b), 1, 1)


def _mm_nt_fwd(a, b):
    a, b = _bf(a), _bf(b)
    return _dg(a, b, 1, 1), (a, b)


def _mm_nt_bwd(res, g):
    a, b = res
    g = _bf(g)
    return _dg(g, b, 1, 0), _dg(g, a, 0, 0)


mm_nt.defvjp(_mm_nt_fwd, _mm_nt_bwd)


@jax.custom_vjp
def mm_tn(a, b):
    return _dg(_bf(a), _bf(b), 0, 0)


def _mm_tn_fwd(a, b):
    a, b = _bf(a), _bf(b)
    return _dg(a, b, 0, 0), (a, b)


def _mm_tn_bwd(res, g):
    a, b = res
    g = _bf(g)
    return _dg(b, g, 1, 1), _dg(a, g, 1, 0)


mm_tn.defvjp(_mm_tn_fwd, _mm_tn_bwd)


def _split2(x):
    hi = _bf(x)
    return hi, _bf(x - hi.astype(F32))


def _split3(x):
    hi = _bf(x)
    r = x - hi.astype(F32)
    mid = _bf(r)
    return hi, mid, _bf(r - mid.astype(F32))


def _dg3(a, b, ca, cb):
    ah, al = _split2(a)
    bh, bl = _split2(b)
    return _dg(ah, bh, ca, cb) + (_dg(ah, bl, ca, cb) + _dg(al, bh, ca, cb))


def _iota(shape, dim):
    return lax.broadcasted_iota(jnp.int32, shape, dim)


def _const_left(c, x, ca):
    hi, mid, lo = _split3(x)
    if x.ndim == 3:
        c = jnp.broadcast_to(c, x.shape[:1] + c.shape)
    return _dg(c, hi, ca, 0) + (_dg(c, mid, ca, 0) + _dg(c, lo, ca, 0))


def _tril_incl(n):
    return (_iota((n, n), 0) >= _iota((n, n), 1)).astype(BF16)


@jax.custom_vjp
def cumsum_rows(x):
    return _const_left(_tril_incl(x.shape[-2]), x, 1)


def _cumsum_rows_fwd(x):
    return cumsum_rows(x), None


def _cumsum_rows_bwd(_, g):
    return (_const_left(_tril_incl(g.shape[-2]), g, 0),)


cumsum_rows.defvjp(_cumsum_rows_fwd, _cumsum_rows_bwd)


def _expand_mat(n_in, first, group, n_out):
    return (_iota((n_in, n_out), 0) == first + _iota((n_in, n_out), 1) // group).astype(BF16)


def _right_const(x, c, cc):
    hi, mid, lo = _split3(x)
    return _dg(hi, c, 1, cc) + (_dg(mid, c, 1, cc) + _dg(lo, c, 1, cc))


def make_expand(first, group, n_out):
    @jax.custom_vjp
    def expand(x):
        return _right_const(x, _expand_mat(x.shape[1], first, group, n_out), 0)

    def fwd(x):
        return expand(x), None

    def bwd(_, g):
        return (_right_const(g, _expand_mat(LANE, first, group, n_out), 1),)

    expand.defvjp(fwd, bwd)
    return expand


def _tri_inv_impl(m):
    n = m.shape[-1]
    r, c = _iota((n, n), 0), _iota((n, n), 1)
    eye = (r == c).astype(F32)
    d = jnp.where(r // 8 == c // 8, m, 0.0)
    t = eye - d
    p = _dg3(d, d, 1, 0)
    t = t + _dg3(t, p, 1, 0)
    p = _dg3(p, p, 1, 0)
    t = t + _dg3(t, p, 1, 0)
    for blk in (16, 32, 64):
        off = jnp.where((r // blk == c // blk) & (r // (blk // 2) != c // (blk // 2)), m, 0.0)
        t = t - _dg3(_dg3(t, off, 1, 0), t, 1, 0)
    return t


@jax.custom_vjp
def tri_inv(m):
    return _tri_inv_impl(m)


def _tri_inv_fwd(m):
    t = _tri_inv_impl(m)
    return t, t


def _tri_inv_bwd(t, g):
    return (-_dg3(_dg3(t, g, 0, 0), t, 1, 1),)


tri_inv.defvjp(_tri_inv_fwd, _tri_inv_bwd)


def _sigmoid(x):
    return 1.0 / (1.0 + jnp.exp(-x))


def _silu(x):
    return x * _sigmoid(x)


def _softplus(x):
    return jnp.maximum(x, 0.0) + jnp.log(1.0 + jnp.exp(-jnp.abs(x)))


def _row_of_diag(cb):
    n = cb.shape[-1]
    return jnp.sum(jnp.where(_iota((n, n), 0) == _iota((n, n), 1), cb, 0.0), axis=-2, keepdims=True)


def gdn_chunk(q, k, v, bx, gx, s):
    n = q.shape[-2]
    gc = cumsum_rows(gx)
    gl = jnp.sum(gx, axis=-2, keepdims=True)
    cb = gc[..., :n]
    seg = cb - _row_of_diag(cb)
    r, c = _iota((n, n), 0), _iota((n, n), 1)
    dec_strict = jnp.exp(jnp.where(r > c, seg, -1e30))
    dec_incl = jnp.exp(jnp.where(r >= c, seg, -1e30))
    kb = k * bx
    t = tri_inv(mm_nt(kb, k) * dec_strict)
    egc = jnp.exp(gc)
    u = mm_nn(t, v * bx)
    w = mm_nn(t, kb * egc)
    aqk = mm_nt(q, k) * dec_incl
    v_new = u - mm_nn(w, s)
    o = mm_nn(q * egc, s) + mm_nn(aqk, v_new)
    s_new = s * jnp.exp(gl) + mm_tn(k * jnp.exp(gl - gc), v_new)
    return o, s_new


def ssd_chunk(xs, la, b, c, st):
    n = xs.shape[-2]
    cs = cumsum_rows(la)
    cl = jnp.sum(la, axis=-2, keepdims=True)
    seg = cs - _row_of_diag(cs)
    r, cc = _iota((n, n), 0), _iota((n, n), 1)
    dec = jnp.exp(jnp.where(r >= cc, seg, -1e30))
    a = mm_nt(c, b) * dec
    y = mm_nn(a, xs) + mm_nn(c, st) * jnp.exp(cs)
    st_new = st * jnp.exp(cl) + mm_tn(b, xs * jnp.exp(cl - cs))
    return y, st_new


def _sds(shape, dtype=F32):
    return jax.ShapeDtypeStruct(shape, dtype)


def _heads(x, n):
    w = x.shape[1] // n
    return jnp.stack([x[:, h * w:(h + 1) * w] for h in range(n)])


def _unheads(x):
    return jnp.concatenate([x[h] for h in range(x.shape[0])], axis=1)


def _bcast(x, n):
    return jnp.broadcast_to(x, (n,) + x.shape)


_GDN_V_BLK = 2 * N_HEADS


def gdn_scan_fwd(q, k, v, bx, gx):
    lp = q.shape[0]
    nc = lp // CHUNK

    def body(q_ref, k_ref, v_ref, b_ref, g_ref, o_ref, st_ref, s_scr):
        @pl.when(pl.program_id(0) == 0)
        def _():
            s_scr[...] = jnp.zeros(s_scr.shape, F32)

        s = s_scr[...]
        st_ref[0] = s
        o, s_new = gdn_chunk(*[_heads(r[...], N_HEADS) for r in (q_ref, k_ref, v_ref, b_ref, g_ref)], s)
        o_ref[...] = _unheads(o)
        s_scr[...] = s_new

    width = N_HEADS * HEAD
    blk = pl.BlockSpec((CHUNK, width), lambda c: (c, 0))
    return pl.pallas_call(
        body, name="gdn_scan_fwd", grid=(nc,),
        in_specs=[blk, blk, pl.BlockSpec((CHUNK, width), lambda c: (c, _GDN_V_BLK // N_HEADS)), blk, blk],
        out_specs=[blk, pl.BlockSpec((1, N_HEADS, HEAD, HEAD), lambda c: (c, 0, 0, 0))],
        out_shape=[_sds((lp, width)), _sds((nc, N_HEADS, HEAD, HEAD))],
        scratch_shapes=[pltpu.VMEM((N_HEADS, HEAD, HEAD), F32)],
        compiler_params=_cparams(dimension_semantics=("arbitrary",)),
    )(q, k, v, bx, gx)


def gdn_scan_bwd(q, k, v, bx, gx, st, do):
    lp = q.shape[0]
    nc = lp // CHUNK

    def body(q_ref, k_ref, v_ref, b_ref, g_ref, st_ref, do_ref, dq_ref, dk_ref, dv_ref, db_ref, dg_ref, ds_scr):
        @pl.when(pl.program_id(0) == 0)
        def _():
            ds_scr[...] = jnp.zeros(ds_scr.shape, F32)

        _, vjp = jax.vjp(gdn_chunk, *[_heads(r[...], N_HEADS) for r in (q_ref, k_ref, v_ref, b_ref, g_ref)],
                         st_ref[0])
        grads = vjp((_heads(do_ref[...], N_HEADS), ds_scr[...]))
        for ref, val in zip((dq_ref, dk_ref, dv_ref, db_ref, dg_ref), grads[:5]):
            ref[...] = _unheads(val)
        ds_scr[...] = grads[5]

    width = N_HEADS * HEAD
    blk = pl.BlockSpec((CHUNK, width), lambda c: (nc - 1 - c, 0))
    return pl.pallas_call(
        body, name="gdn_scan_bwd", grid=(nc,),
        in_specs=[blk, blk, pl.BlockSpec((CHUNK, width), lambda c: (nc - 1 - c, _GDN_V_BLK // N_HEADS)), blk, blk,
                  pl.BlockSpec((1, N_HEADS, HEAD, HEAD), lambda c: (nc - 1 - c, 0, 0, 0)), blk],
        out_specs=[blk] * 5,
        out_shape=[_sds((lp, width))] * 5,
        scratch_shapes=[pltpu.VMEM((N_HEADS, HEAD, HEAD), F32)],
        compiler_params=_cparams(dimension_semantics=("arbitrary",)),
    )(q, k, v, bx, gx, st, do)


SSD_HPS = 8
_SSD_STEPS = SSM_HEADS // SSD_HPS
_XBC_B_BLK = SSM_INNER // LANE
_XBC_C_BLK = _XBC_B_BLK + SSM_GROUPS


def ssd_scan_fwd(xs, la, xbc):
    lp = xs.shape[0]
    nc = lp // CHUNK

    def body(xs_ref, la_ref, b_ref, c_ref, y_ref, st_ref, s_scr):
        j = pl.program_id(1)
        first = pl.multiple_of(j * SSD_HPS, SSD_HPS)

        @pl.when(pl.program_id(0) == 0)
        def _():
            s_scr[pl.ds(first, SSD_HPS)] = jnp.zeros((SSD_HPS, SSM_N, SSM_P), F32)

        s = s_scr[pl.ds(first, SSD_HPS)]
        st_ref[0] = s
        y, s_new = ssd_chunk(_heads(xs_ref[...], SSD_HPS), _heads(la_ref[...], SSD_HPS),
                             _bcast(b_ref[...], SSD_HPS), _bcast(c_ref[...], SSD_HPS), s)
        y_ref[...] = _unheads(y)
        s_scr[pl.ds(first, SSD_HPS)] = s_new

    width = SSD_HPS * SSM_P
    blk = pl.BlockSpec((CHUNK, width), lambda c, j: (c, j))
    per_group = _SSD_STEPS // SSM_GROUPS
    return pl.pallas_call(
        body, name="ssd_scan_fwd", grid=(nc, _SSD_STEPS),
        in_specs=[blk, blk,
                  pl.BlockSpec((CHUNK, LANE), lambda c, j: (c, _XBC_B_BLK + j // per_group)),
                  pl.BlockSpec((CHUNK, LANE), lambda c, j: (c, _XBC_C_BLK + j // per_group))],
        out_specs=[blk, pl.BlockSpec((1, SSD_HPS, SSM_N, SSM_P), lambda c, j: (c, j, 0, 0))],
        out_shape=[_sds((lp, SSM_INNER)), _sds((nc, SSM_HEADS, SSM_N, SSM_P))],
        scratch_shapes=[pltpu.VMEM((SSM_HEADS, SSM_N, SSM_P), F32)],
        compiler_params=_cparams(dimension_semantics=("arbitrary", "arbitrary")),
    )(xs, la, xbc, xbc)


def ssd_scan_bwd(xs, la, xbc, st, dy):
    lp = xs.shape[0]
    nc = lp // CHUNK
    per_group = _SSD_STEPS // SSM_GROUPS

    def body(xs_ref, la_ref, b_ref, c_ref, st_ref, dy_ref, dxs_ref, dla_ref, db_ref, dc_ref, ds_scr):
        j = pl.program_id(1)
        first = pl.multiple_of(j * SSD_HPS, SSD_HPS)

        @pl.when(pl.program_id(0) == 0)
        def _():
            ds_scr[pl.ds(first, SSD_HPS)] = jnp.zeros((SSD_HPS, SSM_N, SSM_P), F32)

        _, vjp = jax.vjp(ssd_chunk, _heads(xs_ref[...], SSD_HPS), _heads(la_ref[...], SSD_HPS),
                         _bcast(b_ref[...], SSD_HPS), _bcast(c_ref[...], SSD_HPS), st_ref[0])
        dxs, dla, db, dc, ds = vjp((_heads(dy_ref[...], SSD_HPS), ds_scr[pl.ds(first, SSD_HPS)]))
        db = jnp.sum(db, axis=0)
        dc = jnp.sum(dc, axis=0)
        dxs_ref[...] = _unheads(dxs)
        dla_ref[...] = _unheads(dla)
        ds_scr[pl.ds(first, SSD_HPS)] = ds

        @pl.when(j % per_group == 0)
        def _():
            db_ref[...] = db
            dc_ref[...] = dc

        @pl.when(j % per_group != 0)
        def _():
            db_ref[...] += db
            dc_ref[...] += dc

    width = SSD_HPS * SSM_P
    blk = pl.BlockSpec((CHUNK, width), lambda c, j: (nc - 1 - c, j))
    return pl.pallas_call(
        body, name="ssd_scan_bwd", grid=(nc, _SSD_STEPS),
        in_specs=[blk, blk,
                  pl.BlockSpec((CHUNK, LANE), lambda c, j: (nc - 1 - c, _XBC_B_BLK + j // per_group)),
                  pl.BlockSpec((CHUNK, LANE), lambda c, j: (nc - 1 - c, _XBC_C_BLK + j // per_group)),
                  pl.BlockSpec((1, SSD_HPS, SSM_N, SSM_P), lambda c, j: (nc - 1 - c, j, 0, 0)), blk],
        out_specs=[blk, blk,
                   pl.BlockSpec((CHUNK, LANE), lambda c, j: (nc - 1 - c, j // per_group)),
                   pl.BlockSpec((CHUNK, LANE), lambda c, j: (nc - 1 - c, j // per_group))],
        out_shape=[_sds((lp, SSM_INNER)), _sds((lp, SSM_INNER)),
                   _sds((lp, SSM_GROUPS * SSM_N)), _sds((lp, SSM_GROUPS * SSM_N))],
        scratch_shapes=[pltpu.VMEM((SSM_HEADS, SSM_N, SSM_P), F32)],
        compiler_params=_cparams(dimension_semantics=("arbitrary", "arbitrary")),
    )(xs, la, xbc, xbc, st, dy)


SB_BLK = 128
SB_HPB = 4
_SB_SCALE = HEAD ** -0.5


def _mm2(x, c):
    hi, lo = _split2(x)
    if x.ndim == 3:
        c = _bcast(c, x.shape[0])
    return _dg(hi, c, 1, 0) + _dg(lo, c, 1, 0)


def _sb_tile(q, kj, diag):
    z = _dg(q, _bf(kj), 1, 1) * _SB_SCALE
    ls = -_softplus(-z)
    r, c = _iota((SB_BLK, SB_BLK), 0), _iota((SB_BLK, SB_BLK), 1)
    valid = jnp.logical_or(jnp.logical_not(diag), c < r)
    lk = ls - z
    return ls, lk, jnp.where(valid, lk, 0.0), valid


def sb_attn_fwd(proj):
    lp = proj.shape[0]
    nq = lp // SB_BLK
    wide = SB_HPB * HEAD

    def body(q_ref, k_ref, v_ref, o_ref, tot_ref):
        i = pl.program_id(1)
        q = _bf(_heads(q_ref[...], SB_HPB))
        r, c = _iota((SB_BLK, SB_BLK), 0), _iota((SB_BLK, SB_BLK), 1)
        later_mat = (r > c).astype(BF16)

        def step(t, carry):
            cs, acc = carry
            j = i - t
            rows = pl.ds(pl.multiple_of(j * SB_BLK, SB_BLK), SB_BLK)
            ls, _, lkm, valid = _sb_tile(q, _heads(k_ref[rows, :], SB_HPB), t == 0)
            w = jnp.where(valid, jnp.exp(ls + _mm2(lkm, later_mat) + cs), 0.0)
            acc = acc + _dg(_bf(w), _bf(_heads(v_ref[rows, :], SB_HPB)), 1, 0)
            return cs + jnp.sum(lkm, axis=-1, keepdims=True), acc

        cs, acc = lax.fori_loop(0, i + 1, step, (jnp.zeros((SB_HPB, SB_BLK, 1), F32),
                                                 jnp.zeros((SB_HPB, SB_BLK, HEAD), F32)))
        o_ref[...] = _unheads(acc)
        tot_ref[...] = _unheads(jnp.broadcast_to(cs, (SB_HPB, SB_BLK, HEAD)))

    out_blk = pl.BlockSpec((SB_BLK, wide), lambda h, i: (i, h))
    return pl.pallas_call(
        body, name="sb_attn_fwd", grid=(N_HEADS // SB_HPB, nq),
        in_specs=[pl.BlockSpec((SB_BLK, wide), lambda h, i: (i, C_SBQ // wide + h)),
                  pl.BlockSpec((lp, wide), lambda h, i: (0, C_SBK // wide + h)),
                  pl.BlockSpec((lp, wide), lambda h, i: (0, C_SBV // wide + h))],
        out_specs=[out_blk, out_blk],
        out_shape=[_sds((lp, N_HEADS * HEAD)), _sds((lp, N_HEADS * HEAD))],
        compiler_params=_cparams(dimension_semantics=("arbitrary", "arbitrary")),
    )(proj, proj, proj)


def sb_attn_bwd(proj, tot, do):
    lp = proj.shape[0]
    nq = lp // SB_BLK
    wide = SB_HPB * HEAD

    def body(q_ref, k_ref, v_ref, tot_ref, do_ref, dq_ref, dk_ref, dv_ref):
        i = pl.program_id(1)

        @pl.when(i == 0)
        def _():
            dk_ref[...] = jnp.zeros((lp, wide), F32)
            dv_ref[...] = jnp.zeros((lp, wide), F32)

        q = _bf(_heads(q_ref[...], SB_HPB))
        do = _bf(_heads(do_ref[...], SB_HPB))
        tot = _heads(tot_ref[...], SB_HPB)
        r, c = _iota((SB_BLK, SB_BLK), 0), _iota((SB_BLK, SB_BLK), 1)
        upto_mat = (r <= c).astype(BF16)
        before_mat = (r < c).astype(BF16)

        def step(j, carry):
            pre, gs, dq = carry
            rows = pl.ds(pl.multiple_of(j * SB_BLK, SB_BLK), SB_BLK)
            kj = _bf(_heads(k_ref[rows, :], SB_HPB))
            vj = _bf(_heads(v_ref[rows, :], SB_HPB))
            ls, lk, lkm, valid = _sb_tile(q, kj, j == i)
            w = jnp.where(valid, jnp.exp(ls + (tot - (pre + _mm2(lkm, upto_mat)))), 0.0)
            g = w * _dg(do, vj, 1, 1)
            dlk = gs + _mm2(g, before_mat)
            dz = _bf(jnp.where(valid, g * jnp.exp(lk) - dlk * jnp.exp(ls), 0.0) * _SB_SCALE)
            dk_ref[rows, :] += _unheads(_dg(dz, q, 0, 0))
            dv_ref[rows, :] += _unheads(_dg(_bf(w), do, 0, 0))
            return (pre + jnp.sum(lkm, axis=-1, keepdims=True), gs + jnp.sum(g, axis=-1, keepdims=True),
                    dq + _dg(dz, kj, 1, 0))

        zero = jnp.zeros((SB_HPB, SB_BLK, 1), F32)
        _, _, dq = lax.fori_loop(0, i + 1, step, (zero, zero, jnp.zeros((SB_HPB, SB_BLK, HEAD), F32)))
        dq_ref[...] = _unheads(dq)

    row_blk = pl.BlockSpec((SB_BLK, wide), lambda h, i: (i, h))
    full_blk = pl.BlockSpec((lp, wide), lambda h, i: (0, h))
    return pl.pallas_call(
        body, name="sb_attn_bwd", grid=(N_HEADS // SB_HPB, nq),
        in_specs=[pl.BlockSpec((SB_BLK, wide), lambda h, i: (i, C_SBQ // wide + h)),
                  pl.BlockSpec((lp, wide), lambda h, i: (0, C_SBK // wide + h)),
                  pl.BlockSpec((lp, wide), lambda h, i: (0, C_SBV // wide + h)),
                  row_blk, row_blk],
        out_specs=[row_blk, full_blk, full_blk],
        out_shape=[_sds((lp, N_HEADS * HEAD))] * 3,
        compiler_params=_cparams(dimension_semantics=("arbitrary", "arbitrary")),
    )(proj, proj, proj, tot, do)


ROW_TILE = 128


def _row_in_specs(rows, consts, tm):
    specs = [pl.BlockSpec((tm, r[2]), functools.partial(lambda i, cb, rb: (rb + i, cb), cb=r[1],
                                                         rb=r[3] if len(r) > 3 else 0)) for r in rows]
    specs += [pl.BlockSpec(c.shape, lambda i: (0, 0)) for c in consts]
    return specs


def rowwise(fn, name, rows, consts, outs, tm=ROW_TILE, n_rows=None):
    n_in = len(rows) + len(consts)
    lp = rows[0][0].shape[0] if n_rows is None else n_rows

    def body(*refs):
        res = fn(*[r[...].astype(F32) for r in refs[:n_in]])
        for o_ref, val in zip(refs[n_in:], res, strict=True):
            o_ref[...] = val.astype(o_ref.dtype)

    return pl.pallas_call(
        body, name=name, grid=(lp // tm,),
        in_specs=_row_in_specs(rows, consts, tm),
        out_specs=[pl.BlockSpec((tm, w), lambda i: (i, 0)) for w, _ in outs],
        out_shape=[_sds((lp, w), dt) for w, dt in outs],
        compiler_params=_cparams(dimension_semantics=("arbitrary",)),
    )(*[r[0] for r in rows], *consts)


def rowwise_vjp(fn, name, rows, consts, cots, d_dtypes, tm=ROW_TILE):
    n_r, n_c, n_o = len(rows), len(consts), len(cots)
    lp = rows[0][0].shape[0]
    wanted = [k for k, dt in enumerate(d_dtypes) if dt is not None]

    def body(*refs):
        ins = [r[...].astype(F32) for r in refs[:n_r + n_c]]
        cot = tuple(r[...].astype(F32) for r in refs[n_r + n_c:n_r + n_c + n_o])
        out_refs = refs[n_r + n_c + n_o:]
        _, vjp = jax.vjp(fn, *ins)
        grads = vjp(cot)
        for o_ref, k in zip(out_refs[:len(wanted)], wanted):
            o_ref[...] = grads[k].astype(o_ref.dtype)

        @pl.when(pl.program_id(0) == 0)
        def _():
            for o_ref in out_refs[len(wanted):]:
                o_ref[...] = jnp.zeros(o_ref.shape, F32)

        for o_ref, g in zip(out_refs[len(wanted):], grads[n_r:], strict=True):
            o_ref[...] += g

    return pl.pallas_call(
        body, name=name, grid=(lp // tm,),
        in_specs=_row_in_specs(rows, consts, tm) + [pl.BlockSpec((tm, c.shape[1]), lambda i: (i, 0)) for c in cots],
        out_specs=[pl.BlockSpec((tm, rows[k][2]), lambda i: (i, 0)) for k in wanted]
        + [pl.BlockSpec(c.shape, lambda i: (0, 0)) for c in consts],
        out_shape=[_sds((lp, rows[k][2]), d_dtypes[k]) for k in wanted] + [_sds(c.shape) for c in consts],
        compiler_params=_cparams(dimension_semantics=("arbitrary",)),
    )(*[r[0] for r in rows], *consts, *cots)


def _rms(t, eps):
    return t * lax.rsqrt(jnp.mean(t * t, axis=-1, keepdims=True) + eps)


def f_rmsnorm(h, g):
    return (_rms(h, RMS_EPS) * g,)


def _per_head(t, fn):
    return jnp.concatenate([fn(t[:, h * HEAD:(h + 1) * HEAD]) for h in range(t.shape[1] // HEAD)], axis=1)


_expand_beta = make_expand(0, HEAD, N_HEADS * HEAD)
_expand_g = make_expand(8, HEAD, N_HEADS * HEAD)
_expand_dt = make_expand(16, SSM_P, SSM_INNER)
_expand_d = make_expand(0, SSM_P, SSM_INNER)


def f_prep(cq, ck, x, misc, bias_row, alog_row):
    def l2(t):
        return t * lax.rsqrt(jnp.sum(t * t, axis=-1, keepdims=True) + L2_EPS)

    qh = _per_head(cq, l2) * (HEAD ** -0.5)
    kh = _per_head(ck, l2)
    step = _softplus(misc + bias_row)
    decay = -jnp.exp(alog_row) * step
    bx = _expand_beta(_sigmoid(misc))
    gx = _expand_g(decay)
    xs = x * _expand_dt(step)
    la = _expand_dt(decay)
    return qh, kh, bx, gx, xs, la


def f_gate(o_a, sb_z, o_b, gd_z, y, x, ss_z, gdn_g, ssm_g, d_row):
    a_in = o_a * _silu(sb_z)
    b_in = _per_head(o_b, lambda t: _rms(t, RMS_EPS) * gdn_g) * _silu(gd_z)
    y2 = (y + _expand_d(d_row) * x) * _silu(ss_z)
    half = SSM_INNER // SSM_GROUPS
    c_in = jnp.concatenate([_rms(y2[:, g * half:(g + 1) * half], RMS_EPS) * ssm_g[:, g * half:(g + 1) * half]
                            for g in range(SSM_GROUPS)], axis=1)
    return a_in, b_in, c_in


def f_mix(pa, pb, pc, ga, gb, gc):
    return (_sigmoid(ga) * pa + _sigmoid(gb) * pb + _sigmoid(gc) * pc,)


def f_add(a, b):
    return (a + b,)


def final_loss(h, tgt, mask, g):
    lp = h.shape[0]
    tm = ROW_TILE

    def body(h_ref, t_ref, m_ref, g_ref, loss_ref, dh_ref, dg_ref):
        tgt, msk = t_ref[...], m_ref[:, :1]

        def f(h, g):
            err = _rms(h, RMS_EPS) * g - tgt
            return jnp.sum(0.5 * jnp.mean(err * err, axis=-1, keepdims=True) * msk, axis=0, keepdims=True)

        val, vjp = jax.vjp(f, h_ref[...], g_ref[...])
        dh, dg = vjp(jnp.ones((1, 1), F32))
        dh_ref[...] = dh

        @pl.when(pl.program_id(0) == 0)
        def _():
            loss_ref[...] = jnp.zeros(loss_ref.shape, F32)
            dg_ref[...] = jnp.zeros(dg_ref.shape, F32)

        loss_ref[...] += jnp.broadcast_to(val, loss_ref.shape)
        dg_ref[...] += dg

    d = h.shape[1]
    return pl.pallas_call(
        body, name="final_loss", grid=(lp // tm,),
        in_specs=[pl.BlockSpec((tm, d), lambda i: (i, 0)), pl.BlockSpec((tm, d), lambda i: (i, 0)),
                  pl.BlockSpec((tm, LANE), lambda i: (i, 0)), pl.BlockSpec((1, d), lambda i: (0, 0))],
        out_specs=[pl.BlockSpec((1, LANE), lambda i: (0, 0)), pl.BlockSpec((tm, d), lambda i: (i, 0)),
                   pl.BlockSpec((1, d), lambda i: (0, 0))],
        out_shape=[_sds((1, LANE)), _sds((lp, d)), _sds((1, d))],
        compiler_params=_cparams(dimension_semantics=("arbitrary",)),
    )(h, tgt, mask, g)


CONV_TILE = 512


def _shift_down(x, s):
    if s == 0:
        return x
    return jnp.where(_iota(x.shape, 0) >= s, pltpu.roll(x, s, 0), 0.0)


def _shift_up(x, s):
    if s == 0:
        return x
    n = x.shape[0]
    return jnp.where(_iota(x.shape, 0) < n - s, pltpu.roll(x, n - s, 0), 0.0)


def _conv_pre(x, w_ref, b):
    pre = b + w_ref[CONV_K - 1:CONV_K, :] * x
    for k in range(CONV_K - 1):
        pre = pre + w_ref[k:k + 1, :] * _shift_down(x, CONV_K - 1 - k)
    return pre


def conv_silu_fwd(x, col, width, w, b, name):
    lp = x.shape[0]
    ct = CONV_TILE if width % CONV_TILE == 0 else LANE * 2

    def body(x_ref, w_ref, b_ref, y_ref):
        y_ref[...] = _silu(_conv_pre(x_ref[...], w_ref, b_ref[...]))

    return pl.pallas_call(
        body, name=name, grid=(width // ct,),
        in_specs=[pl.BlockSpec((lp, ct), lambda j: (0, col // ct + j)),
                  pl.BlockSpec((CONV_K, ct), lambda j: (0, j)), pl.BlockSpec((1, ct), lambda j: (0, j))],
        out_specs=pl.BlockSpec((lp, ct), lambda j: (0, j)),
        out_shape=_sds((lp, width)),
        compiler_params=_cparams(dimension_semantics=("arbitrary",)),
    )(x, w, b)


def conv_silu_bwd(x, col, width, w, b, dy, name):
    lp = x.shape[0]
    ct = CONV_TILE if width % CONV_TILE == 0 else LANE * 2

    def body(x_ref, w_ref, b_ref, dy_ref, dx_ref, dw_ref, db_ref):
        x = x_ref[...]
        pre = _conv_pre(x, w_ref, b_ref[...])
        sg = _sigmoid(pre)
        dpre = dy_ref[...] * (sg * (1.0 + pre * (1.0 - sg)))
        dx = w_ref[CONV_K - 1:CONV_K, :] * dpre
        for k in range(CONV_K - 1):
            dx = dx + w_ref[k:k + 1, :] * _shift_up(dpre, CONV_K - 1 - k)
        dx_ref[...] = dx.astype(dx_ref.dtype)
        for k in range(CONV_K):
            dw_ref[k:k + 1, :] = jnp.sum(dpre * _shift_down(x, CONV_K - 1 - k), axis=0, keepdims=True)
        db_ref[...] = jnp.sum(dpre, axis=0, keepdims=True)

    return pl.pallas_call(
        body, name=name, grid=(width // ct,),
        in_specs=[pl.BlockSpec((lp, ct), lambda j: (0, col // ct + j)),
                  pl.BlockSpec((CONV_K, ct), lambda j: (0, j)), pl.BlockSpec((1, ct), lambda j: (0, j)),
                  pl.BlockSpec((lp, ct), lambda j: (0, j))],
        out_specs=[pl.BlockSpec((lp, ct), lambda j: (0, j)), pl.BlockSpec((CONV_K, ct), lambda j: (0, j)),
                   pl.BlockSpec((1, ct), lambda j: (0, j))],
        out_shape=[_sds((lp, width), BF16), _sds((CONV_K, width)), _sds((1, width))],
        compiler_params=_cparams(dimension_semantics=("arbitrary",)),
    )(x, w, b, dy)


def matmul(a, b, out_dtype, tn, name, add=None, tk=None):
    m, kdim = a.shape
    n = b.shape[1]
    if tk is None:
        def body(a_ref, b_ref, *rest):
            acc = _dg(_bf(a_ref[...]), _bf(b_ref[...]), 1, 0)
            if add is not None:
                acc = acc + rest[0][...]
            rest[-1][...] = acc.astype(out_dtype)

        in_specs = [pl.BlockSpec((m, kdim), lambda j: (0, 0)), pl.BlockSpec((kdim, tn), lambda j: (0, j))]
        if add is not None:
            in_specs.append(pl.BlockSpec((m, tn), lambda j: (0, j)))
        return pl.pallas_call(
            body, name=name, grid=(n // tn,), in_specs=in_specs,
            out_specs=pl.BlockSpec((m, tn), lambda j: (0, j)), out_shape=_sds((m, n), out_dtype),
            compiler_params=_cparams(dimension_semantics=("arbitrary",)),
        )(a, b, *([] if add is None else [add]))

    nk = kdim // tk

    def body_k(a_ref, b_ref, *rest):
        o_ref, acc_ref = rest[-2], rest[-1]
        k = pl.program_id(1)

        @pl.when(k == 0)
        def _():
            acc_ref[...] = jnp.zeros(acc_ref.shape, F32)

        acc_ref[...] += _dg(_bf(a_ref[...]), _bf(b_ref[...]), 1, 0)

        @pl.when(k == nk - 1)
        def _():
            acc = acc_ref[...]
            if add is not None:
                acc = acc + rest[0][...]
            o_ref[...] = acc.astype(out_dtype)

    in_specs = [pl.BlockSpec((m, tk), lambda j, k: (0, k)), pl.BlockSpec((tk, tn), lambda j, k: (k, j))]
    if add is not None:
        in_specs.append(pl.BlockSpec((m, tn), lambda j, k: (0, j)))
    return pl.pallas_call(
        body_k, name=name, grid=(n // tn, nk), in_specs=in_specs,
        out_specs=pl.BlockSpec((m, tn), lambda j, k: (0, j)), out_shape=_sds((m, n), out_dtype),
        scratch_shapes=[pltpu.VMEM((m, tn), F32)],
        compiler_params=_cparams(dimension_semantics=("arbitrary", "arbitrary")),
    )(a, b, *([] if add is None else [add]))


def f_adamw(w, g, m, v):
    m = ADAM_B1 * m + (1.0 - ADAM_B1) * g
    v = ADAM_B2 * v + (1.0 - ADAM_B2) * (g * g)
    m_hat = m / (1.0 - ADAM_B1 ** ADAM_STEP)
    v_hat = v / (1.0 - ADAM_B2 ** ADAM_STEP)
    delta = -ADAM_LR * (m_hat / (jnp.sqrt(v_hat) + ADAM_EPS) + ADAM_WD * w)
    return delta, m, v


def adamw(w, g, m, v, name):
    shape = w.shape
    cols = shape[-1]
    rows = math.prod(shape[:-1]) if len(shape) > 1 else 1
    tm = next((t for t in (256, 128, 64, 32, 16, 8) if rows % t == 0), rows)
    args = [(t.reshape(rows, cols), 0, cols) for t in (w, g, m, v)]
    outs = rowwise(f_adamw, name, args, [], [(cols, F32)] * 3, tm=tm)
    return [o.reshape(shape) for o in outs]


_MESH = pl.DeviceIdType.MESH
_HBM = pl.BlockSpec(memory_space=pltpu.HBM)


def _my_pos():
    return lax.axis_index("x"), lax.axis_index("y"), lax.axis_index("c")


def _linear(px, py, pc):
    return 4 * px + 2 * py + pc


def all_gather(x, name):
    def body(x_ref, out_ref, send_sems, recv_sems, local_sem):
        x, y, c = _my_pos()
        me, sibling = (x, y, c), (x, y, 1 - c)
        chips = [(1 - x, y), (x, 1 - y), (1 - x, 1 - y)]

        def slab(px, py, pc):
            return out_ref.at[_linear(px, py, pc)]

        def copy(k, block, to, src=None):
            return pltpu.make_async_remote_copy(
                src_ref=slab(*block) if src is None else src, dst_ref=slab(*block),
                send_sem=send_sems.at[k], recv_sem=recv_sems.at[k], device_id=to, device_id_type=_MESH)

        mine = pltpu.make_async_copy(x_ref, slab(*me), local_sem)
        mine.start()
        first = [copy(0, me, sibling, src=x_ref)]
        first += [copy(1 + j, me, (*chip, c), src=x_ref) for j, chip in enumerate(chips)]
        for cp in first:
            cp.start()
        passed = [copy(4 + j, (*chip, c), sibling) for j, chip in enumerate(chips)]
        for j, chip in enumerate(chips):
            copy(1 + j, (*chip, c), me).wait_recv()
            passed[j].start()
        copy(0, sibling, me).wait_recv()
        for j, chip in enumerate(chips):
            copy(4 + j, (*chip, 1 - c), me).wait_recv()
        for cp in first + passed:
            cp.wait_send()
        mine.wait()

    return pl.pallas_call(
        body, name=name, out_shape=_sds((N_DEV,) + x.shape, x.dtype), in_specs=[_HBM], out_specs=_HBM,
        scratch_shapes=[pltpu.SemaphoreType.DMA((7,)), pltpu.SemaphoreType.DMA((7,)), pltpu.SemaphoreType.DMA],
    )(x)


def scatter_blocks(g, name):
    def body(g_ref, out_ref, send_sems, recv_sems):
        x, y, c = _my_pos()
        copies = []
        for r in range(1, N_DEV):
            peer = (1 - x if r & 4 else x, 1 - y if r & 2 else y, 1 - c if r & 1 else c)
            copies.append(pltpu.make_async_remote_copy(
                src_ref=g_ref.at[_linear(*peer)], dst_ref=out_ref.at[r - 1],
                send_sem=send_sems.at[r - 1], recv_sem=recv_sems.at[r - 1], device_id=peer, device_id_type=_MESH))
        for cp in copies:
            cp.start()
        for cp in copies:
            cp.wait()

    return pl.pallas_call(
        body, name=name, out_shape=_sds((N_DEV - 1,) + g.shape[1:], g.dtype), in_specs=[_HBM], out_specs=_HBM,
        scratch_shapes=[pltpu.SemaphoreType.DMA((7,)), pltpu.SemaphoreType.DMA((7,))],
    )(g)


def f_sum8(*xs):
    acc = xs[0]
    for t in xs[1:]:
        acc = acc + t
    return (acc,)


W_IN_SHARD = D_IN // N_DEV
PACK_ROWS = 2688
_PACK_SPLITS = (W_IN_SHARD, 128, 128, 256, 128)


def pack_layer(w_in, wa, wb, wc, wo):
    parts = [w_in.reshape(W_IN_SHARD, D_MODEL), wa, wb, wc, wo]
    rows = sum(_PACK_SPLITS)
    return jnp.concatenate(parts + [jnp.zeros((PACK_ROWS - rows, D_MODEL), w_in.dtype)], axis=0)


def unpack_layer(p):
    out, r = [], 0
    for n in _PACK_SPLITS:
        out.append(p[r:r + n])
        r += n
    out[0] = out[0].reshape(D_MODEL, W_IN_SHARD)
    return out


def to_kernel_columns(w):
    k = w.shape[0]
    parts, at = [], 0
    for src, width, dst in sorted(_SEGS, key=lambda seg: seg[2]):
        if dst > at:
            parts.append(jnp.zeros((k, dst - at), w.dtype))
        parts.append(w[:, src:src + width])
        at = dst + width
    if at < D_INC:
        parts.append(jnp.zeros((k, D_INC - at), w.dtype))
    return jnp.concatenate(parts, axis=1)


def from_kernel_columns(w):
    parts = sorted(_SEGS)
    return jnp.concatenate([w[:, dst:dst + width] for _, width, dst in parts], axis=1)


def _lane_row(pieces):
    row = jnp.zeros((1, LANE), F32)
    for first, vec in pieces:
        row = lax.dynamic_update_slice(row, vec.reshape(1, -1).astype(F32), (0, first))
    return row


_D = D_MODEL


def layer_fwd(h, p):
    (u,) = rowwise(f_rmsnorm, "rmsnorm_fwd", [(h, 0, _D)], [p["norm_g"]], [(_D, BF16)])
    proj = matmul(u, p["w_in"], F32, 640, "in_proj")
    cqkv = conv_silu_fwd(proj, C_GDQ, GDN_CONV_CH, p["gdn_conv_w"], jnp.zeros((1, GDN_CONV_CH), F32), "gdn_conv_fwd")
    xbc = conv_silu_fwd(proj, C_XBC, SSM_CONV_CH, p["ssm_conv_w"], p["ssm_conv_b"], "ssm_conv_fwd")
    prep_rows = [(cqkv, 0, 1024), (cqkv, 1, 1024), (xbc, 0, SSM_INNER), (proj, C_MISC // LANE, LANE)]
    prep_consts = [p["bias_row"], p["alog_row"]]
    qh, kh, bx, gx, xs, la = rowwise(f_prep, "prep_fwd", prep_rows, prep_consts,
                                     [(1024, F32)] * 4 + [(SSM_INNER, F32)] * 2)
    o_a, tot = sb_attn_fwd(proj)
    o_b, st_g = gdn_scan_fwd(qh, kh, cqkv, bx, gx)
    y, st_s = ssd_scan_fwd(xs, la, xbc)
    gate_rows = [(o_a, 0, 1024), (proj, C_SBZ // 1024, 1024), (o_b, 0, 1024), (proj, C_GDZ // 1024, 1024),
                 (y, 0, SSM_INNER), (xbc, 0, SSM_INNER), (proj, C_SSZ // SSM_INNER, SSM_INNER)]
    gate_consts = [p["gdn_norm_g"], p["ssm_norm_g"], p["d_row"]]
    a_in, b_in, c_in = rowwise(f_gate, "gate_fwd", gate_rows, gate_consts,
                               [(1024, BF16), (1024, BF16), (SSM_INNER, BF16)])
    pa = matmul(a_in, p["wa"], F32, 512, "branch_a")
    pb = matmul(b_in, p["wb"], F32, 512, "branch_b")
    pc = matmul(c_in, p["wc"], F32, 512, "branch_c")
    mix_rows = [(pa, 0, _D), (pb, 0, _D), (pc, 0, _D)] + [(proj, C_GATE // _D + k, _D) for k in range(3)]
    (merged,) = rowwise(f_mix, "mix_fwd", mix_rows, [], [(_D, BF16)])
    h_new = matmul(merged, p["wo"], F32, 512, "out_proj", add=h)
    saved = dict(h=h, u=u, proj=proj, cqkv=cqkv, xbc=xbc, qh=qh, kh=kh, bx=bx, gx=gx, xs=xs, la=la, o_a=o_a, tot=tot,
                 o_b=o_b, st_g=st_g, y=y, st_s=st_s, a_in=a_in, b_in=b_in, c_in=c_in, pa=pa, pb=pb, pc=pc,
                 merged=merged, prep_rows=prep_rows, prep_consts=prep_consts, gate_rows=gate_rows,
                 gate_consts=gate_consts, mix_rows=mix_rows)
    return h_new, saved


def layer_bwd(dh, s, p):
    proj = s["proj"]
    dmerged = matmul(dh, p["wo_t"], F32, 512, "d_merged")
    g = {"wo": matmul(s["merged"].T, dh, F32, 512, "dw_out")}
    dpa, dpb, dpc, dga, dgb, dgc = rowwise_vjp(f_mix, "mix_bwd", s["mix_rows"], [], [dmerged], [BF16] * 6)
    da_in = matmul(dpa, p["wa_t"], F32, 512, "d_branch_a")
    db_in = matmul(dpb, p["wb_t"], F32, 512, "d_branch_b")
    dc_in = matmul(dpc, p["wc_t"], F32, 512, "d_branch_c")
    g["wa"] = matmul(s["a_in"].T, dpa, F32, 512, "dw_branch_a")
    g["wb"] = matmul(s["b_in"].T, dpb, F32, 512, "dw_branch_b")
    g["wc"] = matmul(s["c_in"].T, dpc, F32, 512, "dw_branch_c")
    (d_oa, d_sbz, d_ob, d_gdz, dy, dx_gate, d_ssz, g["gdn_norm_g"], g["ssm_norm_g"], g["d_row"]) = rowwise_vjp(
        f_gate, "gate_bwd", s["gate_rows"], s["gate_consts"], [da_in, db_in, dc_in],
        [F32, BF16, F32, BF16, F32, F32, BF16])
    dq_sb, dk_sb, dv_sb = sb_attn_bwd(proj, s["tot"], d_oa)
    dqh, dkh, dv_g, dbx, dgx = gdn_scan_bwd(s["qh"], s["kh"], s["cqkv"], s["bx"], s["gx"], s["st_g"], d_ob)
    dxs, dla, db_s, dc_s = ssd_scan_bwd(s["xs"], s["la"], s["xbc"], s["st_s"], dy)
    dcq, dck, dx_prep, dmisc, g["bias_row"], g["alog_row"] = rowwise_vjp(
        f_prep, "prep_bwd", s["prep_rows"], s["prep_consts"], [dqh, dkh, dbx, dgx, dxs, dla], [F32, F32, F32, BF16])
    d_cqkv = jnp.concatenate([dcq, dck, dv_g], axis=1)
    din_g, g["gdn_conv_w"], _ = conv_silu_bwd(proj, C_GDQ, GDN_CONV_CH, p["gdn_conv_w"],
                                              jnp.zeros((1, GDN_CONV_CH), F32), d_cqkv, "gdn_conv_bwd")
    (dx,) = rowwise(f_add, "add_dx", [(dx_gate, 0, SSM_INNER), (dx_prep, 0, SSM_INNER)], [], [(SSM_INNER, F32)])
    d_xbc = jnp.concatenate([dx, db_s, dc_s], axis=1)
    din_s, g["ssm_conv_w"], g["ssm_conv_b"] = conv_silu_bwd(proj, C_XBC, SSM_CONV_CH, p["ssm_conv_w"],
                                                            p["ssm_conv_b"], d_xbc, "ssm_conv_bwd")
    pad = jnp.zeros((proj.shape[0], D_INC - C_MISC - LANE), BF16)
    dproj = jnp.concatenate([_bf(dq_sb), _bf(dk_sb), _bf(dv_sb), d_sbz, din_g, d_gdz, d_ssz, dga, dgb, dgc, din_s,
                             dmisc] + ([pad] if pad.shape[1] else []), axis=1)
    du = matmul(dproj, p["w_in_t"], F32, _D, "d_u", tk=640)
    g["w_in"] = matmul(s["u"].T, dproj, F32, 640, "dw_in")
    dh_norm, g["norm_g"] = rowwise_vjp(f_rmsnorm, "rmsnorm_bwd", [(s["h"], 0, _D)], [p["norm_g"]], [du], [F32])
    (dh_in,) = rowwise(f_add, "add_dh", [(dh, 0, _D), (dh_norm, 0, _D)], [], [(_D, F32)])
    return dh_in, g


_REPLICATED = ("norm_g", "gdn_a_log", "gdn_dt_bias", "gdn_norm_g", "ssm_conv_b", "ssm_a_log", "ssm_dt_bias", "ssm_d",
               "ssm_norm_g", "final_norm_g")
_WEIGHTS = ("meta_tokens", "norm_g", "w_in", "gdn_conv_w", "gdn_a_log", "gdn_dt_bias", "gdn_norm_g", "ssm_conv_w",
            "ssm_conv_b", "ssm_a_log", "ssm_dt_bias", "ssm_d", "ssm_norm_g", "w_branch_a", "w_branch_b", "w_branch_c",
            "w_out", "final_norm_g")
SMALL_ROWS = 128


def _flatten_small(parts):
    flat = jnp.concatenate([t.reshape(-1).astype(F32) for t in parts])
    return jnp.pad(flat, (0, SMALL_ROWS * _D - flat.shape[0])).reshape(SMALL_ROWS, _D)


def _unflatten_small(block, shapes):
    flat, out, at = block.reshape(-1), [], 0
    for shp in shapes:
        n = math.prod(shp)
        out.append(flat[at:at + n].reshape(shp))
        at += n
    return out


def kernel(x, meta_tokens, norm_g, w_in, gdn_conv_w, gdn_a_log, gdn_dt_bias, gdn_norm_g, ssm_conv_w, ssm_conv_b, ssm_a_log, ssm_dt_bias, ssm_d, ssm_norm_g, w_branch_a, w_branch_b, w_branch_c, w_out, final_norm_g, loss_target, m_meta_tokens, m_norm_g, m_w_in, m_gdn_conv_w, m_gdn_a_log, m_gdn_dt_bias, m_gdn_norm_g, m_ssm_conv_w, m_ssm_conv_b, m_ssm_a_log, m_ssm_dt_bias, m_ssm_d, m_ssm_norm_g, m_w_branch_a, m_w_branch_b, m_w_branch_c, m_w_out, m_final_norm_g, v_meta_tokens, v_norm_g, v_w_in, v_gdn_conv_w, v_gdn_a_log, v_gdn_dt_bias, v_gdn_norm_g, v_ssm_conv_w, v_ssm_conv_b, v_ssm_a_log, v_ssm_dt_bias, v_ssm_d, v_ssm_norm_g, v_w_branch_a, v_w_branch_b, v_w_branch_c, v_w_out, v_final_norm_g):
    weights = dict(meta_tokens=meta_tokens, norm_g=norm_g, w_in=w_in, gdn_conv_w=gdn_conv_w, gdn_a_log=gdn_a_log,
                   gdn_dt_bias=gdn_dt_bias, gdn_norm_g=gdn_norm_g, ssm_conv_w=ssm_conv_w, ssm_conv_b=ssm_conv_b,
                   ssm_a_log=ssm_a_log, ssm_dt_bias=ssm_dt_bias, ssm_d=ssm_d, ssm_norm_g=ssm_norm_g,
                   w_branch_a=w_branch_a, w_branch_b=w_branch_b, w_branch_c=w_branch_c, w_out=w_out,
                   final_norm_g=final_norm_g)
    m_in = dict(zip(_WEIGHTS, (m_meta_tokens, m_norm_g, m_w_in, m_gdn_conv_w, m_gdn_a_log, m_gdn_dt_bias, m_gdn_norm_g,
                               m_ssm_conv_w, m_ssm_conv_b, m_ssm_a_log, m_ssm_dt_bias, m_ssm_d, m_ssm_norm_g,
                               m_w_branch_a, m_w_branch_b, m_w_branch_c, m_w_out, m_final_norm_g)))
    v_in = dict(zip(_WEIGHTS, (v_meta_tokens, v_norm_g, v_w_in, v_gdn_conv_w, v_gdn_a_log, v_gdn_dt_bias, v_gdn_norm_g,
                               v_ssm_conv_w, v_ssm_conv_b, v_ssm_a_log, v_ssm_dt_bias, v_ssm_d, v_ssm_norm_g,
                               v_w_branch_a, v_w_branch_b, v_w_branch_c, v_w_out, v_final_norm_g)))
    depth = w_in.shape[0]
    seq = x.shape[1]
    tokens = N_META + seq
    lp = -(-tokens // SB_BLK) * SB_BLK
    me = _linear(*_my_pos())

    packed = jnp.concatenate([pack_layer(_bf(w_in[l]), _bf(w_branch_a[l]), _bf(w_branch_b[l]), _bf(w_branch_c[l]),
                                         _bf(w_out[l])) for l in range(depth)], axis=0)
    big = all_gather(packed, "gather_weights").reshape(N_DEV, depth, PACK_ROWS, _D)
    small_in = jnp.concatenate([gdn_conv_w.reshape(-1), ssm_conv_w.reshape(-1), meta_tokens.reshape(-1)])
    small_in = small_in.reshape(-1, LANE)
    small = all_gather(small_in, "gather_small").reshape(N_DEV, -1)
    n_g, n_s = gdn_conv_w.size, ssm_conv_w.size
    gdn_w_full = small[:, :n_g].reshape((N_DEV,) + gdn_conv_w.shape).transpose(1, 2, 0, 3).reshape(depth, CONV_K, -1)
    ssm_w_full = small[:, n_g:n_g + n_s].reshape((N_DEV,) + ssm_conv_w.shape).transpose(1, 2, 0, 3)
    ssm_w_full = ssm_w_full.reshape(depth, CONV_K, -1)
    meta_full = small[:, n_g + n_s:].reshape(N_DEV, N_META, -1).transpose(1, 0, 2).reshape(N_META, _D)

    params = []
    for l in range(depth):
        blocks = [unpack_layer(big[d, l]) for d in range(N_DEV)]
        w_full = jnp.concatenate([b[0] for b in blocks], axis=1)
        wk = to_kernel_columns(w_full)
        wa, wb, wc, wo = (jnp.concatenate([b[k] for b in blocks], axis=0) for k in range(1, 5))
        params.append(dict(
            norm_g=norm_g[l].reshape(1, _D), w_in=wk, w_in_t=wk.T, gdn_conv_w=gdn_w_full[l], ssm_conv_w=ssm_w_full[l],
            ssm_conv_b=ssm_conv_b[l].reshape(1, -1),
            bias_row=_lane_row([(8, gdn_dt_bias[l]), (16, ssm_dt_bias[l])]),
            alog_row=_lane_row([(8, gdn_a_log[l]), (16, ssm_a_log[l])]),
            gdn_norm_g=gdn_norm_g[l].reshape(1, HEAD), ssm_norm_g=ssm_norm_g[l].reshape(1, SSM_INNER),
            d_row=_lane_row([(0, ssm_d[l])]), wa=wa, wb=wb, wc=wc, wo=wo, wa_t=wa.T, wb_t=wb.T, wc_t=wc.T, wo_t=wo.T))

    h = jnp.concatenate([meta_full, x[0], jnp.zeros((lp - tokens, _D), F32)], axis=0)
    tgt = jnp.pad(loss_target[0], ((N_META, lp - tokens), (0, 0)))
    rows = lax.broadcasted_iota(jnp.int32, (lp, LANE), 0)
    mask = jnp.logical_and(rows >= N_META, rows < tokens).astype(F32)

    saved = []
    for l in range(depth):
        h, s = layer_fwd(h, params[l])
        saved.append(s)
    loss_row, dh, d_final_g = final_loss(h, tgt, mask, final_norm_g.reshape(1, _D))
    grads = [None] * depth
    for l in reversed(range(depth)):
        dh, grads[l] = layer_bwd(dh, saved[l], params[l])

    send = []
    gw_full = [from_kernel_columns(grads[l]["w_in"]) for l in range(depth)]
    for d in range(N_DEV):
        per_layer = []
        for l in range(depth):
            g = grads[l]
            gw = gw_full[l][:, d * W_IN_SHARD:(d + 1) * W_IN_SHARD]
            per_layer.append(pack_layer(gw, g["wa"][d * 128:(d + 1) * 128], g["wb"][d * 128:(d + 1) * 128],
                                        g["wc"][d * 256:(d + 1) * 256], g["wo"][d * 128:(d + 1) * 128]))
        send.append(jnp.concatenate(per_layer, axis=0))
    send = jnp.stack(send)
    own = lax.dynamic_index_in_dim(send, me, axis=0, keepdims=False)
    got = scatter_blocks(_bf(send), "scatter_grads")
    n_rows = depth * PACK_ROWS
    got2 = got.reshape((N_DEV - 1) * n_rows, _D)
    sum_rows = [(own, 0, _D)] + [(got2, 0, _D, r * (n_rows // ROW_TILE)) for r in range(N_DEV - 1)]
    (gsum,) = rowwise(f_sum8, "sum_grads", sum_rows, [], [(_D, F32)], n_rows=n_rows)
    gsum = gsum.reshape(depth, PACK_ROWS, _D)
    big_g = [unpack_layer(gsum[l]) for l in range(depth)]
    grad = dict(w_in=jnp.stack([b[0] for b in big_g]), w_branch_a=jnp.stack([b[1] for b in big_g]),
                w_branch_b=jnp.stack([b[2] for b in big_g]), w_branch_c=jnp.stack([b[3] for b in big_g]),
                w_out=jnp.stack([b[4] for b in big_g]))

    def stack(name, pick=lambda t: t):
        return jnp.stack([pick(grads[l][name]) for l in range(depth)])

    small_parts = [
        stack("norm_g"), stack("alog_row", lambda t: t[0, 8:16]), stack("bias_row", lambda t: t[0, 8:16]),
        stack("gdn_norm_g"), stack("ssm_conv_b"), stack("alog_row", lambda t: t[0, 16:48]),
        stack("bias_row", lambda t: t[0, 16:48]), stack("d_row", lambda t: t[0, :SSM_HEADS]), stack("ssm_norm_g"),
        d_final_g, stack("gdn_conv_w"), stack("ssm_conv_w"), dh[:N_META], loss_row[0, :1]]
    small_shapes = [(depth, _D), (depth, 8), (depth, 8), (depth, HEAD), (depth, SSM_CONV_CH), (depth, SSM_HEADS),
                    (depth, SSM_HEADS), (depth, SSM_HEADS), (depth, SSM_INNER), (_D,), (depth, CONV_K, GDN_CONV_CH),
                    (depth, CONV_K, SSM_CONV_CH), (N_META, _D), ()]
    small_all = all_gather(_flatten_small(small_parts), "gather_small_grads")
    (small_sum,) = rowwise(f_sum8, "sum_small", [(small_all.reshape(N_DEV * SMALL_ROWS, _D), 0, _D, d)
                                                 for d in range(N_DEV)], [], [(_D, F32)], n_rows=SMALL_ROWS)
    (grad["norm_g"], grad["gdn_a_log"], grad["gdn_dt_bias"], grad["gdn_norm_g"], grad["ssm_conv_b"],
     grad["ssm_a_log"], grad["ssm_dt_bias"], grad["ssm_d"], grad["ssm_norm_g"], grad["final_norm_g"], gdn_cw, ssm_cw,
     meta_g, loss) = _unflatten_small(small_sum, small_shapes)
    grad["gdn_conv_w"] = lax.dynamic_slice_in_dim(gdn_cw, me * gdn_conv_w.shape[2], gdn_conv_w.shape[2], axis=2)
    grad["ssm_conv_w"] = lax.dynamic_slice_in_dim(ssm_cw, me * ssm_conv_w.shape[2], ssm_conv_w.shape[2], axis=2)
    grad["meta_tokens"] = lax.dynamic_slice_in_dim(meta_g, me * meta_tokens.shape[1], meta_tokens.shape[1], axis=1)

    delta, new_m, new_v = {}, {}, {}
    for name in _WEIGHTS:
        delta[name], new_m[name], new_v[name] = adamw(weights[name], grad[name], m_in[name], v_in[name],
                                                      "adamw_" + name)
    grad_x = dh[N_META:tokens][None]
    return (loss, grad_x, *[grad[n] for n in _WEIGHTS], *[delta[n] for n in _WEIGHTS],
            *[new_m[n] for n in _WEIGHTS], *[new_v[n] for n in _WEIGHTS])
```
